```python
import math
import jax, jax.numpy as jnp
from jax import lax
import numpy as np

D_MODEL = 2048
BATCH = 16
SEQ = 2048
DEPTH = 2

N_META = 16
Q_BLOCK = 128
HEAD_DIM = 128
FOX_HEADS = 6
FOX_W = FOX_HEADS * HEAD_DIM
DIFF_HEADS = 4
DIFF_QK = 64
DIFF_V = 128
DIFF_W = DIFF_HEADS * DIFF_V
CONV_CH = 768
CONV_WIDTH = 31
MIX_W = FOX_W + DIFF_W + CONV_CH
D_FF = 5632
FFN_CONV = 3
EPS = 1e-6
SUBLN_EPS = 1e-5
FOX_SCALE = HEAD_DIM ** -0.5
DIFF_SCALE = DIFF_QK ** -0.5

O_FQ = 0
O_FK = O_FQ + FOX_W
O_FV = O_FK + FOX_W
O_FF = O_FV + FOX_W
O_DQ = O_FF + FOX_HEADS
O_DK = O_DQ + DIFF_HEADS * 2 * DIFF_QK
O_DV = O_DK + DIFF_HEADS * 2 * DIFF_QK
O_CG = O_DV + DIFF_W
N_IN = O_CG + 2 * CONV_CH

kernel_name = "hymba_fox_conformer_diffattn_block"


def rmsnorm(x, g, eps=EPS):
    xf = x.astype(jnp.float32)
    y = xf * lax.rsqrt(jnp.mean(xf * xf, axis=-1, keepdims=True) + eps)
    return (y * g.astype(jnp.float32)).astype(x.dtype)


def layernorm(x, g, b):
    xf = x.astype(jnp.float32)
    mu = jnp.mean(xf, axis=-1, keepdims=True)
    xc = xf - mu
    y = xc * lax.rsqrt(jnp.mean(xc * xc, axis=-1, keepdims=True) + EPS)
    return (y * g.astype(jnp.float32) + b.astype(jnp.float32)).astype(x.dtype)


def causal_dwconv(x, w):
    k = w.shape[0]
    return lax.conv_general_dilated(
        x, w[:, None, :].astype(x.dtype), window_strides=(1,), padding=[(k - 1, 0)],
        dimension_numbers=("NWC", "WIO", "NWC"), feature_group_count=x.shape[-1])


def query_blocks(length):
    return [(0, N_META)] + [(s, min(s + Q_BLOCK, length)) for s in range(N_META, length, Q_BLOCK)]


def causal_mask(start, end):
    return jnp.arange(end)[None, :] <= jnp.arange(start, end)[:, None]


def fox_attention(q, k, v, log_f):
    c = jnp.transpose(jnp.cumsum(log_f, axis=1), (0, 2, 1))
    outs = []
    for start, end in query_blocks(q.shape[1]):
        s = jnp.einsum("bqhd,bkhd->bhqk", q[:, start:end], k[:, :end],
                       preferred_element_type=jnp.float32) * FOX_SCALE
        s = s + c[:, :, start:end, None] - c[:, :, None, :end]
        s = jnp.where(causal_mask(start, end), s, -jnp.inf)
        p = jax.nn.softmax(s, axis=-1)
        outs.append(jnp.einsum("bhqk,bkhd->bqhd", p.astype(v.dtype), v[:, :end]))
    return jnp.concatenate(outs, axis=1)


def diff_attention(q1, q2, k1, k2, v, lam):
    outs = []
    for start, end in query_blocks(q1.shape[1]):
        mask = causal_mask(start, end)
        s1 = jnp.einsum("bqhd,bkhd->bhqk", q1[:, start:end], k1[:, :end],
                        preferred_element_type=jnp.float32) * DIFF_SCALE
        s2 = jnp.einsum("bqhd,bkhd->bhqk", q2[:, start:end], k2[:, :end],
                        preferred_element_type=jnp.float32) * DIFF_SCALE
        p1 = jax.nn.softmax(jnp.where(mask, s1, -jnp.inf), axis=-1)
        p2 = jax.nn.softmax(jnp.where(mask, s2, -jnp.inf), axis=-1)
        p = p1 - lam * p2
        outs.append(jnp.einsum("bhqk,bkhd->bqhd", p.astype(v.dtype), v[:, :end]))
    return jnp.concatenate(outs, axis=1)


def hybrid_mixer(h, layer, w_in, b_f, lam_q1, lam_k1, lam_q2, lam_k2, g_sub,
                 w_dw, b_dw, ln_g, ln_b, w_out):
    bsz, length, _ = h.shape
    proj = h @ w_in

    fq = proj[..., O_FQ:O_FK].reshape(bsz, length, FOX_HEADS, HEAD_DIM)
    fk = proj[..., O_FK:O_FV].reshape(bsz, length, FOX_HEADS, HEAD_DIM)
    fv = proj[..., O_FV:O_FF].reshape(bsz, length, FOX_HEADS, HEAD_DIM)
    log_f = jax.nn.log_sigmoid((proj[..., O_FF:O_DQ] + b_f).astype(jnp.float32))
    y_fox = fox_attention(fq, fk, fv, log_f).reshape(bsz, length, FOX_W)

    dq = proj[..., O_DQ:O_DK].reshape(bsz, length, DIFF_HEADS, 2, DIFF_QK)
    dk = proj[..., O_DK:O_DV].reshape(bsz, length, DIFF_HEADS, 2, DIFF_QK)
    dv = proj[..., O_DV:O_CG].reshape(bsz, length, DIFF_HEADS, DIFF_V)
    lam_init = 0.8 - 0.6 * math.exp(-0.3 * layer)
    lam = (jnp.exp(jnp.sum(lam_q1.astype(jnp.float32) * lam_k1.astype(jnp.float32)))
           - jnp.exp(jnp.sum(lam_q2.astype(jnp.float32) * lam_k2.astype(jnp.float32)))
           + lam_init)
    y_diff = diff_attention(dq[..., 0, :], dq[..., 1, :], dk[..., 0, :], dk[..., 1, :], dv, lam)
    y_diff = (rmsnorm(y_diff, g_sub, SUBLN_EPS) * (1.0 - lam_init)).reshape(bsz, length, DIFF_W)

    a = proj[..., O_CG:O_CG + CONV_CH]
    gt = proj[..., O_CG + CONV_CH:N_IN]
    u = a * jax.nn.sigmoid(gt)
    u = causal_dwconv(u, w_dw) + b_dw
    y_conv = jax.nn.silu(layernorm(u, ln_g, ln_b))

    y = jnp.concatenate([y_fox, y_diff, y_conv], axis=-1)
    return y @ w_out


def conv_gated_ffn(h, w_gate, w_up, w_conv, w_down):
    g = causal_dwconv(h @ w_gate, w_conv)
    return (jax.nn.silu(g) * (h @ w_up)) @ w_down


def setup_inputs(seed: int = 0) -> dict:
    key = jax.random.key(seed)
    ks = jax.random.split(key, 24)
    f32 = jnp.float32

    def nrm(k, shape, scale):
        return jax.random.normal(k, shape, f32) * scale

    def gain(k, shape):
        return 1.0 + 0.05 * jax.random.normal(k, shape, f32)

    return {
        "x": nrm(ks[0], (BATCH, SEQ, D_MODEL), 1.0),
        "meta_tokens": nrm(ks[1], (N_META, D_MODEL), 1.0),
        "w_in": nrm(ks[2], (DEPTH, D_MODEL, N_IN), D_MODEL ** -0.5),
        "b_f": nrm(ks[3], (DEPTH, FOX_HEADS), 0.1),
        "lam_q1": nrm(ks[4], (DEPTH, DIFF_QK), 0.1),
        "lam_k1": nrm(ks[5], (DEPTH, DIFF_QK), 0.1),
        "lam_q2": nrm(ks[6], (DEPTH, DIFF_QK), 0.1),
        "lam_k2": nrm(ks[7], (DEPTH, DIFF_QK), 0.1),
        "g_sub": gain(ks[8], (DEPTH, DIFF_V)),
        "w_dw": nrm(ks[9], (DEPTH, CONV_WIDTH, CONV_CH), CONV_WIDTH ** -0.5),
        "b_dw": nrm(ks[10], (DEPTH, CONV_CH), 0.02),
        "ln_g": gain(ks[11], (DEPTH, CONV_CH)),
        "ln_b": nrm(ks[12], (DEPTH, CONV_CH), 0.02),
        "w_out": nrm(ks[13], (DEPTH, MIX_W, D_MODEL), MIX_W ** -0.5),
        "w_gate": nrm(ks[14], (DEPTH, D_MODEL, D_FF), D_MODEL ** -0.5),
        "w_up": nrm(ks[15], (DEPTH, D_MODEL, D_FF), D_MODEL ** -0.5),
        "w_ffn_conv": nrm(ks[16], (DEPTH, FFN_CONV, D_FF), FFN_CONV ** -0.5),
        "w_down": nrm(ks[17], (DEPTH, D_FF, D_MODEL), D_FF ** -0.5),
        "g_pre_mix": gain(ks[18], (DEPTH, D_MODEL)),
        "g_post_mix": gain(ks[19], (DEPTH, D_MODEL)),
        "g_pre_ffn": gain(ks[20], (DEPTH, D_MODEL)),
        "g_post_ffn": gain(ks[21], (DEPTH, D_MODEL)),
    }


def reference(x, meta_tokens, w_in, b_f, lam_q1, lam_k1, lam_q2, lam_k2, g_sub,
              w_dw, b_dw, ln_g, ln_b, w_out, w_gate, w_up, w_ffn_conv, w_down,
              g_pre_mix, g_post_mix, g_pre_ffn, g_post_ffn):
    bsz = x.shape[0]
    meta = jnp.broadcast_to(meta_tokens.astype(x.dtype)[None], (bsz, N_META, x.shape[-1]))
    h = jnp.concatenate([meta, x], axis=1)
    for layer in range(DEPTH):
        m = hybrid_mixer(rmsnorm(h, g_pre_mix[layer]), layer, w_in[layer], b_f[layer],
                         lam_q1[layer], lam_k1[layer], lam_q2[layer], lam_k2[layer],
                         g_sub[layer], w_dw[layer], b_dw[layer], ln_g[layer], ln_b[layer],
                         w_out[layer])
        h = h + rmsnorm(m, g_post_mix[layer])
        f = conv_gated_ffn(rmsnorm(h, g_pre_ffn[layer]), w_gate[layer], w_up[layer],
                           w_ffn_conv[layer], w_down[layer])
        h = h + rmsnorm(f, g_post_ffn[layer])
    return h[:, N_META:]
```

```python
import functools
import math

import jax
import jax.numpy as jnp
from jax import lax
from jax.experimental import pallas as pl
from jax.experimental.pallas import tpu as pltpu

D_MODEL = 2048
N_META = 16
HEAD_DIM = 128
FOX_HEADS = 6
FOX_W = FOX_HEADS * HEAD_DIM
DIFF_HEADS = 4
DIFF_QK = 64
DIFF_V = 128
DIFF_W = DIFF_HEADS * DIFF_V
CONV_CH = 768
CONV_WIDTH = 31
D_FF = 5632
FFN_CONV = 3
EPS = 1e-6
SUBLN_EPS = 1e-5
FOX_SCALE = HEAD_DIM ** -0.5
DIFF_SCALE = DIFF_QK ** -0.5

O_FF = 3 * FOX_W
O_DQ = O_FF + FOX_HEADS

PACK_W = 3 * FOX_W + 3 * DIFF_W + 2 * CONV_CH
BLK_FK, BLK_FV, BLK_DQ, BLK_DK, BLK_DV = 6, 12, 18, 22, 26
CONV_A_BLK, CONV_G_BLK = 5, 6

LANES = 128
ROW_TILE = 688
HALO = 16
PROJ_TN = 768
FFN_TN = 512
ATT_T = 256
MASK_VALUE = -1e30
VMEM_LIMIT = 56 * 1024 * 1024

F32 = jnp.float32
BF16 = jnp.bfloat16


def _rms_rows(x, g, eps):
    return (x * lax.rsqrt(jnp.mean(x * x, axis=-1, keepdims=True) + eps)) * g


def _sigmoid(x):
    return 1.0 / (1.0 + jnp.exp(-x))


def _dot_nt(a, b):
    return lax.dot_general(a, b, (((1,), (1,)), ((), ())), preferred_element_type=F32)


def _inproj_kernel(h_ref, g_ref, w_ref, wff_ref, proj_ref, zf_ref, hn_ref):
    @pl.when(pl.program_id(1) == 0)
    def _():
        hn = _rms_rows(h_ref[...], g_ref[...], EPS).astype(BF16)
        hn_ref[...] = hn
        zf_ref[...] = jnp.dot(hn, wff_ref[...], preferred_element_type=F32)

    proj_ref[...] = jnp.dot(hn_ref[...], w_ref[...], preferred_element_type=F32).astype(BF16)


def _inproj(h, g, w_pack, w_ff):
    t = h.shape[0]
    grid = (t // ROW_TILE, PACK_W // PROJ_TN)
    return pl.pallas_call(
        _inproj_kernel,
        grid=grid,
        in_specs=[
            pl.BlockSpec((ROW_TILE, D_MODEL), lambda i, j: (i, 0)),
            pl.BlockSpec((1, D_MODEL), lambda i, j: (0, 0)),
            pl.BlockSpec((D_MODEL, PROJ_TN), lambda i, j: (0, j)),
            pl.BlockSpec((D_MODEL, LANES), lambda i, j: (0, 0)),
        ],
        out_specs=[
            pl.BlockSpec((ROW_TILE, PROJ_TN), lambda i, j: (i, j)),
            pl.BlockSpec((ROW_TILE, LANES), lambda i, j: (i, 0)),
        ],
        out_shape=[
            jax.ShapeDtypeStruct((t, PACK_W), BF16),
            jax.ShapeDtypeStruct((t, LANES), F32),
        ],
        scratch_shapes=[pltpu.VMEM((ROW_TILE, D_MODEL), BF16)],
        compiler_params=pltpu.CompilerParams(
            dimension_semantics=("arbitrary", "arbitrary"), vmem_limit_bytes=VMEM_LIMIT),
        name="inproj",
    )(h, g, w_pack, w_ff)


def _lane_cumsum(x):
    lane = lax.broadcasted_iota(jnp.int32, x.shape, 1)
    d = 1
    while d < LANES:
        x = x + jnp.where(lane >= d, pltpu.roll(x, d, axis=1), 0.0)
        d *= 2
    return x


def _forget_kernel(zf_ref, bf_ref, c_ref, *, seq):
    n_full = seq // LANES
    carry = jnp.zeros((8, 1), F32)
    for cidx in range(n_full + 1):
        r0 = cidx * LANES
        rows = min(LANES, seq - r0)
        z = zf_ref[0, r0:r0 + rows, :] + bf_ref[...]
        lf = jnp.minimum(z, 0.0) - jnp.log(1.0 + jnp.exp(-jnp.abs(z)))
        if rows < LANES:
            lf = jnp.concatenate([lf, jnp.zeros((LANES - rows, LANES), F32)], axis=0)
        lft = lf.T[0:8, :]
        cs = _lane_cumsum(lft) + carry
        carry = cs[:, LANES - 1:LANES]
        tile, off = divmod(r0, ATT_T)
        c_ref[0, tile, :, off:off + LANES] = cs
        if rows < LANES and off + LANES < ATT_T:
            c_ref[0, tile, :, off + LANES:ATT_T] = jnp.zeros((8, ATT_T - off - LANES), F32)


def _forget_cumsum(zf3, bf_pad):
    b, seq, _ = zf3.shape
    n_tiles = -(-seq // ATT_T)
    return pl.pallas_call(
        functools.partial(_forget_kernel, seq=seq),
        grid=(b,),
        in_specs=[
            pl.BlockSpec((1, seq, LANES), lambda i: (i, 0, 0)),
            pl.BlockSpec((1, LANES), lambda i: (0, 0)),
        ],
        out_specs=pl.BlockSpec((1, n_tiles, 8, ATT_T), lambda i: (i, 0, 0, 0)),
        out_shape=jax.ShapeDtypeStruct((b, n_tiles, 8, ATT_T), F32),
        compiler_params=pltpu.CompilerParams(dimension_semantics=("arbitrary",)),
        name="forget_cumsum",
    )(zf3, bf_pad)


def _online_step(q, k, v, bias, mask, state, scale):
    m, l, acc = state
    s = _dot_nt(q, k) * scale
    if bias is not None:
        s = s + bias
    if mask is not None:
        s = jnp.where(mask, s, MASK_VALUE)
    m_new = jnp.maximum(m, jnp.max(s, axis=-1, keepdims=True))
    alpha = jnp.exp(m - m_new)
    p = jnp.exp(s - m_new)
    l = alpha * l + jnp.sum(p, axis=-1, keepdims=True)
    acc = alpha * acc + jnp.dot(p.astype(BF16), v, preferred_element_type=F32)
    return m_new, l, acc


def _causal_rows(q_of, k_ref, v_ref, bias_of, rows, n_stack, seq, scale, emit):
    n_full = seq // ATT_T
    tail = seq - n_full * ATT_T

    def init(n):
        return (jnp.full((n_stack * n, 1), MASK_VALUE, F32),
                jnp.zeros((n_stack * n, 1), F32),
                jnp.zeros((n_stack * n, HEAD_DIM), F32))

    def tri_mask(n, nk):
        r = lax.broadcasted_iota(jnp.int32, (n, nk), 0)
        c = lax.broadcasted_iota(jnp.int32, (n, nk), 1)
        mk = r >= c
        return jnp.concatenate([mk] * n_stack, axis=0) if n_stack > 1 else mk

    def full_step(q):
        def body(j, state):
            k0 = pl.multiple_of(j * ATT_T, ATT_T)
            k = k_ref[0, pl.ds(k0, ATT_T), :]
            v = v_ref[0, pl.ds(k0, ATT_T), :]
            return _online_step(q, k, v, bias_of(j, ATT_T), None, state, scale)
        return body

    def q_tile(i, _):
        r0 = pl.multiple_of(i * ATT_T, ATT_T)
        q = q_of(r0, ATT_T)
        state = lax.fori_loop(0, i, full_step(q), init(ATT_T))
        k = k_ref[0, pl.ds(r0, ATT_T), :]
        v = v_ref[0, pl.ds(r0, ATT_T), :]
        _, l, acc = _online_step(q, k, v, bias_of(i, ATT_T), tri_mask(ATT_T, ATT_T), state, scale)
        emit(r0, ATT_T, l, acc)
        return 0

    lax.fori_loop(0, n_full, q_tile, 0)

    if tail:
        r0 = n_full * ATT_T
        q = q_of(r0, tail)
        state = lax.fori_loop(0, n_full, full_step(q), init(tail))
        k = k_ref[0, r0:r0 + tail, :]
        v = v_ref[0, r0:r0 + tail, :]
        _, l, acc = _online_step(q, k, v, bias_of(n_full, tail), tri_mask(tail, tail), state, scale)
        emit(r0, tail, l, acc)


def _fox_kernel(q_ref, k_ref, v_ref, c_ref, o_ref, *, seq):
    head = pl.program_id(1)

    def q_of(r0, n):
        return q_ref[0, pl.ds(r0, n), :]

    def bias_of(tile, n):
        return -c_ref[0, tile, pl.ds(head, 1), :][:, 0:n]

    def emit(r0, n, l, acc):
        o_ref[0, pl.ds(r0, n), :] = (acc / l).astype(BF16)

    _causal_rows(q_of, k_ref, v_ref, bias_of, ATT_T, 1, seq, FOX_SCALE, emit)


def _fox_attention(proj3, c4):
    b, seq, _ = proj3.shape
    n_tiles = c4.shape[1]
    blk = (1, seq, HEAD_DIM)
    return pl.pallas_call(
        functools.partial(_fox_kernel, seq=seq),
        grid=(b, FOX_HEADS),
        in_specs=[
            pl.BlockSpec(blk, lambda i, h: (i, 0, h)),
            pl.BlockSpec(blk, lambda i, h: (i, 0, BLK_FK + h)),
            pl.BlockSpec(blk, lambda i, h: (i, 0, BLK_FV + h)),
            pl.BlockSpec((1, n_tiles, 8, ATT_T), lambda i, h: (i, 0, 0, 0)),
        ],
        out_specs=pl.BlockSpec(blk, lambda i, h: (i, 0, h)),
        out_shape=jax.ShapeDtypeStruct((b, seq, FOX_W), BF16),
        compiler_params=pltpu.CompilerParams(dimension_semantics=("arbitrary", "arbitrary")),
        name="fox_attention",
    )(proj3, proj3, proj3, c4)


def _diff_kernel(q_ref, k_ref, v_ref, lq1_ref, lk1_ref, lq2_ref, lk2_ref, gsub_ref, o_ref,
                 *, seq, lam_init):
    lam = (jnp.exp(jnp.sum(lq1_ref[...] * lk1_ref[...], axis=-1, keepdims=True))
           - jnp.exp(jnp.sum(lq2_ref[...] * lk2_ref[...], axis=-1, keepdims=True))
           + lam_init)

    def q_of(r0, n):
        q = q_ref[0, pl.ds(r0, n), :]
        lane = lax.broadcasted_iota(jnp.int32, q.shape, 1)
        zero = jnp.zeros_like(q)
        return jnp.concatenate([jnp.where(lane < DIFF_QK, q, zero),
                                jnp.where(lane >= DIFF_QK, q, zero)], axis=0)

    def emit(r0, n, l, acc):
        o = acc / l
        y = o[0:n] - lam * o[n:2 * n]
        y = _rms_rows(y, gsub_ref[...], SUBLN_EPS) * (1.0 - lam_init)
        o_ref[0, pl.ds(r0, n), :] = y.astype(BF16)

    _causal_rows(q_of, k_ref, v_ref, lambda tile, n: None, ATT_T, 2, seq, DIFF_SCALE, emit)


def _diff_attention(proj3, lq1, lk1, lq2, lk2, g_sub, lam_init):
    b, seq, _ = proj3.shape
    blk = (1, seq, HEAD_DIM)
    vec = pl.BlockSpec((1, DIFF_QK), lambda i, h: (0, 0))
    return pl.pallas_call(
        functools.partial(_diff_kernel, seq=seq, lam_init=lam_init),
        grid=(b, DIFF_HEADS),
        in_specs=[
            pl.BlockSpec(blk, lambda i, h: (i, 0, BLK_DQ + h)),
            pl.BlockSpec(blk, lambda i, h: (i, 0, BLK_DK + h)),
            pl.BlockSpec(blk, lambda i, h: (i, 0, BLK_DV + h)),
            vec, vec, vec, vec,
            pl.BlockSpec((1, DIFF_V), lambda i, h: (0, 0)),
        ],
        out_specs=pl.BlockSpec(blk, lambda i, h: (i, 0, h)),
        out_shape=jax.ShapeDtypeStruct((b, seq, DIFF_W), BF16),
        compiler_params=pltpu.CompilerParams(dimension_semantics=("arbitrary", "arbitrary")),
        name="diff_attention",
    )(proj3, proj3, proj3, lq1, lk1, lq2, lk2, g_sub)


SUBLANES = 8
CONV_PAD = 32
CONV_CHUNK = 48


def _conv_kernel(a_ref, g_ref, w_ref, b_ref, lng_ref, lnb_ref, o_ref, u_ref, y_ref, *, seq):
    n_chunks = seq // CONV_CHUNK
    win = CONV_CHUNK + SUBLANES
    u_ref[0:CONV_PAD, :] = jnp.zeros((CONV_PAD, CONV_CH), F32)

    def glu(c, _):
        r0 = pl.multiple_of(c * CONV_CHUNK, CONV_CHUNK)
        a = a_ref[0, pl.ds(r0, CONV_CHUNK), :].astype(F32)
        g = g_ref[0, pl.ds(r0, CONV_CHUNK), :].astype(F32)
        u_ref[pl.ds(r0 + CONV_PAD, CONV_CHUNK), :] = a * _sigmoid(g)
        return 0

    lax.fori_loop(0, n_chunks, glu, 0)

    def chunk(c, _):
        r0 = pl.multiple_of(c * CONV_CHUNK, CONV_CHUNK)
        for lc in range(CONV_CH // LANES):
            cols = slice(lc * LANES, (lc + 1) * LANES)
            xs = [u_ref[pl.ds(r0 + CONV_PAD - SUBLANES * (a + 1), win), cols]
                  for a in range(-(-CONV_WIDTH // SUBLANES))]
            acc = None
            for r in range(SUBLANES):
                part = None
                for a, x in enumerate(xs):
                    s = SUBLANES * a + r
                    if s >= CONV_WIDTH:
                        continue
                    k = CONV_WIDTH - 1 - s
                    term = x * w_ref[k:k + 1, cols]
                    part = term if part is None else part + term
                if r:
                    part = pltpu.roll(part, r, axis=0)
                part = part[SUBLANES:, :]
                acc = part if acc is None else acc + part
            y_ref[:, cols] = acc
        u = y_ref[...] + b_ref[...]
        mu = jnp.mean(u, axis=-1, keepdims=True)
        uc = u - mu
        y = uc * lax.rsqrt(jnp.mean(uc * uc, axis=-1, keepdims=True) + EPS)
        y = y * lng_ref[...] + lnb_ref[...]
        o_ref[0, pl.ds(r0, CONV_CHUNK), :] = (y * _sigmoid(y)).astype(BF16)
        return 0

    lax.fori_loop(0, n_chunks, chunk, 0)


def _conv_module(proj3, w_dw, b_dw, ln_g, ln_b):
    b, seq, _ = proj3.shape
    assert seq % CONV_CHUNK == 0
    vec = pl.BlockSpec((1, CONV_CH), lambda i: (0, 0))
    return pl.pallas_call(
        functools.partial(_conv_kernel, seq=seq),
        grid=(b,),
        in_specs=[
            pl.BlockSpec((1, seq, CONV_CH), lambda i: (i, 0, CONV_A_BLK)),
            pl.BlockSpec((1, seq, CONV_CH), lambda i: (i, 0, CONV_G_BLK)),
            pl.BlockSpec((CONV_WIDTH, CONV_CH), lambda i: (0, 0)),
            vec, vec, vec,
        ],
        out_specs=pl.BlockSpec((1, seq, CONV_CH), lambda i: (i, 0, 0)),
        out_shape=jax.ShapeDtypeStruct((b, seq, CONV_CH), BF16),
        scratch_shapes=[
            pltpu.VMEM((CONV_PAD + seq, CONV_CH), F32),
            pltpu.VMEM((CONV_CHUNK, CONV_CH), F32),
        ],
        compiler_params=pltpu.CompilerParams(
            dimension_semantics=("arbitrary",), vmem_limit_bytes=VMEM_LIMIT),
        name="conv_module",
    )(proj3, proj3, w_dw, b_dw, ln_g, ln_b)


def _outproj_kernel(yf_ref, yd_ref, yc_ref, w_ref, h_ref, g_ref, o_ref):
    m = jnp.dot(yf_ref[...], w_ref[0:FOX_W, :], preferred_element_type=F32)
    m = m + jnp.dot(yd_ref[...], w_ref[FOX_W:FOX_W + DIFF_W, :], preferred_element_type=F32)
    m = m + jnp.dot(yc_ref[...], w_ref[FOX_W + DIFF_W:, :], preferred_element_type=F32)
    o_ref[...] = h_ref[...] + _rms_rows(m, g_ref[...], EPS)


def _outproj(y_fox, y_diff, y_conv, w_out, h, g):
    t = h.shape[0]
    rows = lambda w: pl.BlockSpec((ROW_TILE, w), lambda i: (i, 0))
    return pl.pallas_call(
        _outproj_kernel,
        grid=(t // ROW_TILE,),
        in_specs=[
            rows(FOX_W), rows(DIFF_W), rows(CONV_CH),
            pl.BlockSpec((D_MODEL, D_MODEL), lambda i: (0, 0)),
            rows(D_MODEL),
            pl.BlockSpec((1, D_MODEL), lambda i: (0, 0)),
        ],
        out_specs=rows(D_MODEL),
        out_shape=jax.ShapeDtypeStruct((t, D_MODEL), F32),
        compiler_params=pltpu.CompilerParams(
            dimension_semantics=("arbitrary",), vmem_limit_bytes=VMEM_LIMIT),
        name="outproj",
    )(y_fox, y_diff, y_conv, w_out, h, g)


def _ffn_kernel(h_ref, hp_ref, gpre_ref, wg_ref, wu_ref, wc_ref, wd_ref, gpost_ref, o_ref,
                hn_ref, gate_ref, acc_ref, *, tiles_per_seq):
    i, j = pl.program_id(0), pl.program_id(1)

    @pl.when(j == 0)
    def _():
        keep = jnp.where(i % tiles_per_seq == 0, 0.0, 1.0)
        hn_ref[0:HALO, :] = (_rms_rows(hp_ref[...], gpre_ref[...], EPS) * keep).astype(BF16)
        hn_ref[HALO:, :] = _rms_rows(h_ref[...], gpre_ref[...], EPS).astype(BF16)
        acc_ref[...] = jnp.zeros_like(acc_ref)

    gate_ref[...] = jnp.dot(hn_ref[...], wg_ref[...], preferred_element_type=F32)
    up = jnp.dot(hn_ref[HALO:, :], wu_ref[...], preferred_element_type=F32)
    conv = gate_ref[HALO:, :] * wc_ref[FFN_CONV - 1:FFN_CONV, :]
    for k in range(FFN_CONV - 1):
        back = FFN_CONV - 1 - k
        conv = conv + gate_ref[HALO - back:HALO - back + ROW_TILE, :] * wc_ref[k:k + 1, :]
    act = (conv * _sigmoid(conv) * up).astype(BF16)
    acc_ref[...] += jnp.dot(act, wd_ref[...], preferred_element_type=F32)

    @pl.when(j == pl.num_programs(1) - 1)
    def _():
        o_ref[...] = h_ref[...] + _rms_rows(acc_ref[...], gpost_ref[...], EPS)


def _ffn(h, g_pre, w_gate, w_up, w_conv, w_down, g_post, seq):
    t = h.shape[0]
    per = ROW_TILE // HALO
    return pl.pallas_call(
        functools.partial(_ffn_kernel, tiles_per_seq=seq // ROW_TILE),
        grid=(t // ROW_TILE, D_FF // FFN_TN),
        in_specs=[
            pl.BlockSpec((ROW_TILE, D_MODEL), lambda i, j: (i, 0)),
            pl.BlockSpec((HALO, D_MODEL), lambda i, j: (jnp.maximum(i * per - 1, 0), 0)),
            pl.BlockSpec((1, D_MODEL), lambda i, j: (0, 0)),
            pl.BlockSpec((D_MODEL, FFN_TN), lambda i, j: (0, j)),
            pl.BlockSpec((D_MODEL, FFN_TN), lambda i, j: (0, j)),
            pl.BlockSpec((FFN_CONV, FFN_TN), lambda i, j: (0, j)),
            pl.BlockSpec((FFN_TN, D_MODEL), lambda i, j: (j, 0)),
            pl.BlockSpec((1, D_MODEL), lambda i, j: (0, 0)),
        ],
        out_specs=pl.BlockSpec((ROW_TILE, D_MODEL), lambda i, j: (i, 0)),
        out_shape=jax.ShapeDtypeStruct((t, D_MODEL), F32),
        scratch_shapes=[
            pltpu.VMEM((HALO + ROW_TILE, D_MODEL), BF16),
            pltpu.VMEM((HALO + ROW_TILE, FFN_TN), F32),
            pltpu.VMEM((ROW_TILE, D_MODEL), F32),
        ],
        compiler_params=pltpu.CompilerParams(
            dimension_semantics=("arbitrary", "arbitrary"), vmem_limit_bytes=VMEM_LIMIT),
        name="ffn",
    )(h, h, g_pre, w_gate, w_up, w_conv, w_down, g_post)


def kernel(x, meta_tokens, w_in, b_f, lam_q1, lam_k1, lam_q2, lam_k2, g_sub, w_dw, b_dw, ln_g,
           ln_b, w_out, w_gate, w_up, w_ffn_conv, w_down, g_pre_mix, g_post_mix, g_pre_ffn,
           g_post_ffn):
    bsz, _, d = x.shape
    depth = w_in.shape[0]
    meta = jnp.broadcast_to(meta_tokens.astype(x.dtype)[None], (bsz, N_META, d))
    h = jnp.concatenate([meta, x], axis=1)
    seq = h.shape[1]
    assert seq % ROW_TILE == 0 and ROW_TILE % HALO == 0
    h = h.reshape(bsz * seq, d)

    row = lambda v: v.reshape(1, -1)
    for layer in range(depth):
        w_l = w_in[layer]
        w_pack = jnp.concatenate([w_l[:, :O_FF], w_l[:, O_DQ:]], axis=1).astype(BF16)
        w_ff = jnp.pad(w_l[:, O_FF:O_DQ], ((0, 0), (0, LANES - FOX_HEADS))).astype(BF16)
        bf_pad = jnp.pad(b_f[layer], (0, LANES - FOX_HEADS)).reshape(1, LANES)
        lam_init = 0.8 - 0.6 * math.exp(-0.3 * layer)

        proj, zf = _inproj(h, row(g_pre_mix[layer]), w_pack, w_ff)
        proj3 = proj.reshape(bsz, seq, PACK_W)
        c4 = _forget_cumsum(zf.reshape(bsz, seq, LANES), bf_pad)
        y_fox = _fox_attention(proj3, c4).reshape(bsz * seq, FOX_W)
        y_diff = _diff_attention(proj3, row(lam_q1[layer]), row(lam_k1[layer]),
                                 row(lam_q2[layer]), row(lam_k2[layer]), row(g_sub[layer]),
                                 lam_init).reshape(bsz * seq, DIFF_W)
        y_conv = _conv_module(proj3, w_dw[layer], row(b_dw[layer]), row(ln_g[layer]),
                              row(ln_b[layer])).reshape(bsz * seq, CONV_CH)
        h = _outproj(y_fox, y_diff, y_conv, w_out[layer].astype(BF16), h, row(g_post_mix[layer]))
        h = _ffn(h, row(g_pre_ffn[layer]), w_gate[layer].astype(BF16), w_up[layer].astype(BF16),
                 w_ffn_conv[layer], w_down[layer].astype(BF16), row(g_post_ffn[layer]), seq)

    return h.reshape(bsz, seq, d)[:, N_META:]
```

```python
import functools
import math

import jax
import jax.numpy as jnp
from jax import lax
from jax.experimental import pallas as pl
from jax.experimental.pallas import tpu as pltpu

D_MODEL = 2048
N_META = 16
HEAD_DIM = 128
FOX_HEADS = 6
FOX_W = FOX_HEADS * HEAD_DIM
DIFF_HEADS = 4
DIFF_QK = 64
DIFF_V = 128
DIFF_W = DIFF_HEADS * DIFF_V
CONV_CH = 768
CONV_WIDTH = 31
D_FF = 5632
FFN_CONV = 3
EPS = 1e-6
SUBLN_EPS = 1e-5
FOX_SCALE = HEAD_DIM ** -0.5
DIFF_SCALE = DIFF_QK ** -0.5

O_FF = 3 * FOX_W
O_DQ = O_FF + FOX_HEADS

O_DV_END = O_DQ + 3 * DIFF_W
N_IN = O_DV_END + 2 * CONV_CH

PACK_W = 3 * DIFF_W + 3 * FOX_W + 2 * CONV_CH
P_FOX = 3 * DIFF_W
P_CONV = P_FOX + 3 * FOX_W
CONV_A_BLK, CONV_G_BLK = P_CONV // CONV_CH, P_CONV // CONV_CH + 1

LANES = 128
ROW_TILE = 688
HALO = 16
PROJ_TN = 768
FFN_TN = 512
ATT_T = 256
FOX_G = 3
DIFF_G = 2
LOG2E = math.log2(math.e)
MASK_VALUE = -1e30
VMEM_LIMIT = 56 * 1024 * 1024

F32 = jnp.float32
BF16 = jnp.bfloat16


def _rms_rows(x, g, eps):
    return (x * lax.rsqrt(jnp.mean(x * x, axis=-1, keepdims=True) + eps)) * g


def _sigmoid(x):
    return 1.0 / (1.0 + jnp.exp(-x))


def _dot_nt(a, b):
    return lax.dot_general(a, b, (((1,), (1,)), ((), ())), preferred_element_type=F32)


PACK_ROWS = 256


def _pack_kernel(w_ref, wp_ref, wff_ref):
    wp_ref[0, :, 0:P_FOX] = w_ref[0, :, O_DQ:O_DV_END].astype(BF16)
    wp_ref[0, :, P_FOX:P_CONV] = w_ref[0, :, 0:O_FF].astype(BF16)
    wp_ref[0, :, P_CONV:PACK_W] = w_ref[0, :, O_DV_END:N_IN].astype(BF16)
    ff = w_ref[0, :, O_FF:O_FF + LANES]
    lane = lax.broadcasted_iota(jnp.int32, ff.shape, 1)
    wff_ref[0] = jnp.where(lane < FOX_HEADS, ff, 0.0).astype(BF16)


def _pack_w_in(w_in):
    depth, d, n_in = w_in.shape
    assert n_in == N_IN
    return pl.pallas_call(
        _pack_kernel,
        grid=(depth, d // PACK_ROWS),
        in_specs=[pl.BlockSpec((1, PACK_ROWS, n_in), lambda l, r: (l, r, 0))],
        out_specs=[
            pl.BlockSpec((1, PACK_ROWS, PACK_W), lambda l, r: (l, r, 0)),
            pl.BlockSpec((1, PACK_ROWS, LANES), lambda l, r: (l, r, 0)),
        ],
        out_shape=[
            jax.ShapeDtypeStruct((depth, d, PACK_W), BF16),
            jax.ShapeDtypeStruct((depth, d, LANES), BF16),
        ],
        compiler_params=pltpu.CompilerParams(dimension_semantics=("arbitrary", "arbitrary")),
        name="pack_w_in",
    )(w_in)


def _inproj_kernel(h_ref, g_ref, w_ref, wff_ref, proj_ref, zf_ref, hn_ref):
    @pl.when(pl.program_id(1) == 0)
    def _():
        hn = _rms_rows(h_ref[...], g_ref[...], EPS).astype(BF16)
        hn_ref[...] = hn
        zf_ref[...] = jnp.dot(hn, wff_ref[...], preferred_element_type=F32)

    proj_ref[...] = jnp.dot(hn_ref[...], w_ref[...], preferred_element_type=F32).astype(BF16)


def _inproj(h, g, w_pack, w_ff, layer):
    t = h.shape[0]
    grid = (t // ROW_TILE, PACK_W // PROJ_TN)
    return pl.pallas_call(
        _inproj_kernel,
        grid=grid,
        in_specs=[
            pl.BlockSpec((ROW_TILE, D_MODEL), lambda i, j: (i, 0)),
            pl.BlockSpec((1, D_MODEL), lambda i, j: (0, 0)),
            pl.BlockSpec((None, D_MODEL, PROJ_TN), lambda i, j: (layer, 0, j)),
            pl.BlockSpec((None, D_MODEL, LANES), lambda i, j: (layer, 0, 0)),
        ],
        out_specs=[
            pl.BlockSpec((ROW_TILE, PROJ_TN), lambda i, j: (i, j)),
            pl.BlockSpec((ROW_TILE, LANES), lambda i, j: (i, 0)),
        ],
        out_shape=[
            jax.ShapeDtypeStruct((t, PACK_W), BF16),
            jax.ShapeDtypeStruct((t, LANES), F32),
        ],
        scratch_shapes=[pltpu.VMEM((ROW_TILE, D_MODEL), BF16)],
        compiler_params=pltpu.CompilerParams(
            dimension_semantics=("arbitrary", "arbitrary"), vmem_limit_bytes=VMEM_LIMIT),
        name="inproj",
    )(h, g, w_pack, w_ff)


def _lane_cumsum(x):
    lane = lax.broadcasted_iota(jnp.int32, x.shape, 1)
    d = 1
    while d < LANES:
        x = x + jnp.where(lane >= d, pltpu.roll(x, d, axis=1), 0.0)
        d *= 2
    return x


def _forget_kernel(zf_ref, bf_ref, c_ref, *, seq):
    n_full = seq // LANES
    carry = jnp.zeros((8, 1), F32)
    for cidx in range(n_full + 1):
        r0 = cidx * LANES
        rows = min(LANES, seq - r0)
        z = zf_ref[0, r0:r0 + rows, :] + bf_ref[...]
        lf = jnp.minimum(z, 0.0) - jnp.log(1.0 + jnp.exp(-jnp.abs(z)))
        if rows < LANES:
            lf = jnp.concatenate([lf, jnp.zeros((LANES - rows, LANES), F32)], axis=0)
        lft = lf.T[0:8, :]
        cs = _lane_cumsum(lft) + carry
        carry = cs[:, LANES - 1:LANES]
        tile, off = divmod(r0, ATT_T)
        c_ref[0, tile, :, off:off + LANES] = cs
        if rows < LANES and off + LANES < ATT_T:
            c_ref[0, tile, :, off + LANES:ATT_T] = jnp.zeros((8, ATT_T - off - LANES), F32)


def _forget_cumsum(zf3, bf_pad):
    b, seq, _ = zf3.shape
    n_tiles = -(-seq // ATT_T)
    return pl.pallas_call(
        functools.partial(_forget_kernel, seq=seq),
        grid=(b,),
        in_specs=[
            pl.BlockSpec((1, seq, LANES), lambda i: (i, 0, 0)),
            pl.BlockSpec((1, LANES), lambda i: (0, 0)),
        ],
        out_specs=pl.BlockSpec((1, n_tiles, 8, ATT_T), lambda i: (i, 0, 0, 0)),
        out_shape=jax.ShapeDtypeStruct((b, n_tiles, 8, ATT_T), F32),
        compiler_params=pltpu.CompilerParams(dimension_semantics=("arbitrary",)),
        name="forget_cumsum",
    )(zf3, bf_pad)


def _online_update(s, v, bias, mask, state, scale2):
    m, l, acc = state
    s = s * scale2
    if bias is not None:
        s = s + bias
    if mask is not None:
        s = jnp.where(mask, s, MASK_VALUE)
    m_new = jnp.maximum(m, jnp.max(s, axis=-1, keepdims=True))
    alpha = jnp.exp2(m - m_new)
    p = jnp.exp2(s - m_new)
    l = alpha * l + jnp.sum(p, axis=-1, keepdims=True)
    acc = alpha * acc + jnp.dot(p.astype(BF16), v, preferred_element_type=F32)
    return m_new, l, acc


def _causal_attention(n_chains, n_stack, seq, scale2, q_of, kv_of, bias_of, emit):
    n_full = seq // ATT_T
    tail = seq - n_full * ATT_T
    chains = range(n_chains)

    def init(n):
        return tuple((jnp.full((n_stack * n, 1), MASK_VALUE, F32),
                      jnp.zeros((n_stack * n, 1), F32),
                      jnp.zeros((n_stack * n, HEAD_DIM), F32)) for _ in chains)

    def tri_mask(n):
        r = lax.broadcasted_iota(jnp.int32, (n, n), 0)
        c = lax.broadcasted_iota(jnp.int32, (n, n), 1)
        mk = r >= c
        return jnp.concatenate([mk] * n_stack, axis=0) if n_stack > 1 else mk

    def run_tile(r0, n, tile, n_before):
        qs = [q_of(g, r0, n) for g in chains]

        def scores(k0, nk):
            return tuple(_dot_nt(qs[g], kv_of(g, k0, nk)[0]) for g in chains)

        def update(ss, t, k0, nk, mask, states):
            return tuple(_online_update(ss[g], kv_of(g, k0, nk)[1], bias_of(g, t, nk), mask,
                                        states[g], scale2) for g in chains)

        def body(j, carry):
            ss, states = carry
            nxt = scores(pl.multiple_of((j + 1) * ATT_T, ATT_T), ATT_T)
            states = update(ss, j, pl.multiple_of(j * ATT_T, ATT_T), ATT_T, None, states)
            return nxt, states

        states = init(n)
        if n == ATT_T:
            ss, states = lax.fori_loop(0, n_before, body, (scores(0, ATT_T), states))
        else:
            ss, states = lax.fori_loop(0, n_before - 1, body, (scores(0, ATT_T), states))
            diag = scores(r0, n)
            last = n_before - 1
            states = update(ss, last, last * ATT_T, ATT_T, None, states)
            ss = diag
        states = update(ss, tile, r0, n, tri_mask(n), states)
        for g in chains:
            emit(g, r0, n, states[g][1], states[g][2])

    def q_tile(i, _):
        run_tile(pl.multiple_of(i * ATT_T, ATT_T), ATT_T, i, i)
        return 0

    lax.fori_loop(0, n_full, q_tile, 0)
    if tail:
        run_tile(n_full * ATT_T, tail, n_full, n_full)


def _head_cols(g):
    return slice(g * HEAD_DIM, (g + 1) * HEAD_DIM)


def _fox_kernel(q_ref, k_ref, v_ref, c_ref, o_ref, *, seq, heads):
    head0 = pl.program_id(1) * heads

    def q_of(g, r0, n):
        return q_ref[0, pl.ds(r0, n), _head_cols(g)]

    def kv_of(g, k0, n):
        return k_ref[0, pl.ds(k0, n), _head_cols(g)], v_ref[0, pl.ds(k0, n), _head_cols(g)]

    def bias_of(g, tile, n):
        return (c_ref[0, tile, pl.ds(head0 + g, 1), :] * (-LOG2E))[:, 0:n]

    def emit(g, r0, n, l, acc):
        o_ref[0, pl.ds(r0, n), _head_cols(g)] = (acc / l).astype(BF16)

    _causal_attention(heads, 1, seq, FOX_SCALE * LOG2E, q_of, kv_of, bias_of, emit)


def _fox_attention(proj3, c4):
    b, seq, _ = proj3.shape
    n_tiles = c4.shape[1]
    width = FOX_G * HEAD_DIM
    blk = (1, seq, width)
    base = P_FOX // width
    per = FOX_W // width
    return pl.pallas_call(
        functools.partial(_fox_kernel, seq=seq, heads=FOX_G),
        grid=(b, per),
        in_specs=[
            pl.BlockSpec(blk, lambda i, h: (i, 0, base + h)),
            pl.BlockSpec(blk, lambda i, h: (i, 0, base + per + h)),
            pl.BlockSpec(blk, lambda i, h: (i, 0, base + 2 * per + h)),
            pl.BlockSpec((1, n_tiles, 8, ATT_T), lambda i, h: (i, 0, 0, 0)),
        ],
        out_specs=pl.BlockSpec(blk, lambda i, h: (i, 0, h)),
        out_shape=jax.ShapeDtypeStruct((b, seq, FOX_W), BF16),
        compiler_params=pltpu.CompilerParams(
            dimension_semantics=("arbitrary", "arbitrary"), vmem_limit_bytes=VMEM_LIMIT),
        name="fox_attention",
    )(proj3, proj3, proj3, c4)


def _diff_kernel(q_ref, k_ref, v_ref, lq1_ref, lk1_ref, lq2_ref, lk2_ref, gsub_ref, o_ref,
                 *, seq, heads, lam_init):
    lam = (jnp.exp(jnp.sum(lq1_ref[...] * lk1_ref[...], axis=-1, keepdims=True))
           - jnp.exp(jnp.sum(lq2_ref[...] * lk2_ref[...], axis=-1, keepdims=True))
           + lam_init)

    def q_of(g, r0, n):
        q = q_ref[0, pl.ds(r0, n), _head_cols(g)]
        lane = lax.broadcasted_iota(jnp.int32, q.shape, 1)
        zero = jnp.zeros_like(q)
        return jnp.concatenate([jnp.where(lane < DIFF_QK, q, zero),
                                jnp.where(lane >= DIFF_QK, q, zero)], axis=0)

    def kv_of(g, k0, n):
        return k_ref[0, pl.ds(k0, n), _head_cols(g)], v_ref[0, pl.ds(k0, n), _head_cols(g)]

    def emit(g, r0, n, l, acc):
        o = acc / l
        y = o[0:n] - lam * o[n:2 * n]
        y = _rms_rows(y, gsub_ref[...], SUBLN_EPS) * (1.0 - lam_init)
        o_ref[0, pl.ds(r0, n), _head_cols(g)] = y.astype(BF16)

    _causal_attention(heads, 2, seq, DIFF_SCALE * LOG2E, q_of, kv_of,
                      lambda g, tile, n: None, emit)


def _diff_attention(proj3, lq1, lk1, lq2, lk2, g_sub, lam_init):
    b, seq, _ = proj3.shape
    width = DIFF_G * HEAD_DIM
    blk = (1, seq, width)
    per = DIFF_W // width
    vec = pl.BlockSpec((1, DIFF_QK), lambda i, h: (0, 0))
    return pl.pallas_call(
        functools.partial(_diff_kernel, seq=seq, heads=DIFF_G, lam_init=lam_init),
        grid=(b, per),
        in_specs=[
            pl.BlockSpec(blk, lambda i, h: (i, 0, h)),
            pl.BlockSpec(blk, lambda i, h: (i, 0, per + h)),
            pl.BlockSpec(blk, lambda i, h: (i, 0, 2 * per + h)),
            vec, vec, vec, vec,
            pl.BlockSpec((1, DIFF_V), lambda i, h: (0, 0)),
        ],
        out_specs=pl.BlockSpec(blk, lambda i, h: (i, 0, h)),
        out_shape=jax.ShapeDtypeStruct((b, seq, DIFF_W), BF16),
        compiler_params=pltpu.CompilerParams(
            dimension_semantics=("arbitrary", "arbitrary"), vmem_limit_bytes=VMEM_LIMIT),
        name="diff_attention",
    )(proj3, proj3, proj3, lq1, lk1, lq2, lk2, g_sub)


SUBLANES = 8
CONV_PAD = 32
CONV_CHUNK = 48


def _conv_kernel(a_ref, g_ref, w_ref, b_ref, lng_ref, lnb_ref, o_ref, u_ref, y_ref, *, seq):
    n_chunks = seq // CONV_CHUNK
    win = CONV_CHUNK + SUBLANES
    u_ref[0:CONV_PAD, :] = jnp.zeros((CONV_PAD, CONV_CH), F32)

    def glu(c, _):
        r0 = pl.multiple_of(c * CONV_CHUNK, CONV_CHUNK)
        a = a_ref[0, pl.ds(r0, CONV_CHUNK), :].astype(F32)
        g = g_ref[0, pl.ds(r0, CONV_CHUNK), :].astype(F32)
        u_ref[pl.ds(r0 + CONV_PAD, CONV_CHUNK), :] = a * _sigmoid(g)
        return 0

    lax.fori_loop(0, n_chunks, glu, 0)

    def chunk(c, _):
        r0 = pl.multiple_of(c * CONV_CHUNK, CONV_CHUNK)
        for lc in range(CONV_CH // LANES):
            cols = slice(lc * LANES, (lc + 1) * LANES)
            xs = [u_ref[pl.ds(r0 + CONV_PAD - SUBLANES * (a + 1), win), cols]
                  for a in range(-(-CONV_WIDTH // SUBLANES))]
            acc = None
            for r in range(SUBLANES):
                part = None
                for a, x in enumerate(xs):
                    s = SUBLANES * a + r
                    if s >= CONV_WIDTH:
                        continue
                    k = CONV_WIDTH - 1 - s
                    term = x * w_ref[k:k + 1, cols]
                    part = term if part is None else part + term
                if r:
                    part = pltpu.roll(part, r, axis=0)
                part = part[SUBLANES:, :]
                acc = part if acc is None else acc + part
            y_ref[:, cols] = acc
        u = y_ref[...] + b_ref[...]
        mu = jnp.mean(u, axis=-1, keepdims=True)
        uc = u - mu
        y = uc * lax.rsqrt(jnp.mean(uc * uc, axis=-1, keepdims=True) + EPS)
        y = y * lng_ref[...] + lnb_ref[...]
        o_ref[0, pl.ds(r0, CONV_CHUNK), :] = (y * _sigmoid(y)).astype(BF16)
        return 0

    lax.fori_loop(0, n_chunks, chunk, 0)


def _conv_module(proj3, w_dw, b_dw, ln_g, ln_b):
    b, seq, _ = proj3.shape
    assert seq % CONV_CHUNK == 0
    vec = pl.BlockSpec((1, CONV_CH), lambda i: (0, 0))
    return pl.pallas_call(
        functools.partial(_conv_kernel, seq=seq),
        grid=(b,),
        in_specs=[
            pl.BlockSpec((1, seq, CONV_CH), lambda i: (i, 0, CONV_A_BLK)),
            pl.BlockSpec((1, seq, CONV_CH), lambda i: (i, 0, CONV_G_BLK)),
            pl.BlockSpec((CONV_WIDTH, CONV_CH), lambda i: (0, 0)),
            vec, vec, vec,
        ],
        out_specs=pl.BlockSpec((1, seq, CONV_CH), lambda i: (i, 0, 0)),
        out_shape=jax.ShapeDtypeStruct((b, seq, CONV_CH), BF16),
        scratch_shapes=[
            pltpu.VMEM((CONV_PAD + seq, CONV_CH), F32),
            pltpu.VMEM((CONV_CHUNK, CONV_CH), F32),
        ],
        compiler_params=pltpu.CompilerParams(
            dimension_semantics=("arbitrary",), vmem_limit_bytes=VMEM_LIMIT),
        name="conv_module",
    )(proj3, proj3, w_dw, b_dw, ln_g, ln_b)


def _outproj_kernel(yf_ref, yd_ref, yc_ref, w_ref, h_ref, g_ref, o_ref):
    m = jnp.dot(yf_ref[...], w_ref[0:FOX_W, :], preferred_element_type=F32)
    m = m + jnp.dot(yd_ref[...], w_ref[FOX_W:FOX_W + DIFF_W, :], preferred_element_type=F32)
    m = m + jnp.dot(yc_ref[...], w_ref[FOX_W + DIFF_W:, :], preferred_element_type=F32)
    o_ref[...] = h_ref[...] + _rms_rows(m, g_ref[...], EPS)


def _outproj(y_fox, y_diff, y_conv, w_out, h, g):
    t = h.shape[0]
    rows = lambda w: pl.BlockSpec((ROW_TILE, w), lambda i: (i, 0))
    return pl.pallas_call(
        _outproj_kernel,
        grid=(t // ROW_TILE,),
        in_specs=[
            rows(FOX_W), rows(DIFF_W), rows(CONV_CH),
            pl.BlockSpec((D_MODEL, D_MODEL), lambda i: (0, 0)),
            rows(D_MODEL),
            pl.BlockSpec((1, D_MODEL), lambda i: (0, 0)),
        ],
        out_specs=rows(D_MODEL),
        out_shape=jax.ShapeDtypeStruct((t, D_MODEL), F32),
        compiler_params=pltpu.CompilerParams(
            dimension_semantics=("arbitrary",), vmem_limit_bytes=VMEM_LIMIT),
        name="outproj",
    )(y_fox, y_diff, y_conv, w_out, h, g)


def _ffn_kernel(h_ref, hp_ref, gpre_ref, wg_ref, wu_ref, wc_ref, wd_ref, gpost_ref, o_ref,
                hn_ref, gate_ref, acc_ref, *, tiles_per_seq):
    i, j = pl.program_id(0), pl.program_id(1)

    @pl.when(j == 0)
    def _():
        keep = jnp.where(i % tiles_per_seq == 0, 0.0, 1.0)
        hn_ref[0:HALO, :] = (_rms_rows(hp_ref[...], gpre_ref[...], EPS) * keep).astype(BF16)
        hn_ref[HALO:, :] = _rms_rows(h_ref[...], gpre_ref[...], EPS).astype(BF16)
        acc_ref[...] = jnp.zeros_like(acc_ref)

    gate_ref[...] = jnp.dot(hn_ref[...], wg_ref[...], preferred_element_type=F32)
    up = jnp.dot(hn_ref[HALO:, :], wu_ref[...], preferred_element_type=F32)
    conv = gate_ref[HALO:, :] * wc_ref[FFN_CONV - 1:FFN_CONV, :]
    for k in range(FFN_CONV - 1):
        back = FFN_CONV - 1 - k
        conv = conv + gate_ref[HALO - back:HALO - back + ROW_TILE, :] * wc_ref[k:k + 1, :]
    act = (conv * _sigmoid(conv) * up).astype(BF16)
    acc_ref[...] += jnp.dot(act, wd_ref[...], preferred_element_type=F32)

    @pl.when(j == pl.num_programs(1) - 1)
    def _():
        o_ref[...] = h_ref[...] + _rms_rows(acc_ref[...], gpost_ref[...], EPS)


def _ffn(h, g_pre, w_gate, w_up, w_conv, w_down, g_post, seq):
    t = h.shape[0]
    per = ROW_TILE // HALO
    return pl.pallas_call(
        functools.partial(_ffn_kernel, tiles_per_seq=seq // ROW_TILE),
        grid=(t // ROW_TILE, D_FF // FFN_TN),
        in_specs=[
            pl.BlockSpec((ROW_TILE, D_MODEL), lambda i, j: (i, 0)),
            pl.BlockSpec((HALO, D_MODEL), lambda i, j: (jnp.maximum(i * per - 1, 0), 0)),
            pl.BlockSpec((1, D_MODEL), lambda i, j: (0, 0)),
            pl.BlockSpec((D_MODEL, FFN_TN), lambda i, j: (0, j)),
            pl.BlockSpec((D_MODEL, FFN_TN), lambda i, j: (0, j)),
            pl.BlockSpec((FFN_CONV, FFN_TN), lambda i, j: (0, j)),
            pl.BlockSpec((FFN_TN, D_MODEL), lambda i, j: (j, 0)),
            pl.BlockSpec((1, D_MODEL), lambda i, j: (0, 0)),
        ],
        out_specs=pl.BlockSpec((ROW_TILE, D_MODEL), lambda i, j: (i, 0)),
        out_shape=jax.ShapeDtypeStruct((t, D_MODEL), F32),
        scratch_shapes=[
            pltpu.VMEM((HALO + ROW_TILE, D_MODEL), BF16),
            pltpu.VMEM((HALO + ROW_TILE, FFN_TN), F32),
            pltpu.VMEM((ROW_TILE, D_MODEL), F32),
        ],
        compiler_params=pltpu.CompilerParams(
            dimension_semantics=("arbitrary", "arbitrary"), vmem_limit_bytes=VMEM_LIMIT),
        name="ffn",
    )(h, h, g_pre, w_gate, w_up, w_conv, w_down, g_post)


def kernel(x, meta_tokens, w_in, b_f, lam_q1, lam_k1, lam_q2, lam_k2, g_sub, w_dw, b_dw, ln_g,
           ln_b, w_out, w_gate, w_up, w_ffn_conv, w_down, g_pre_mix, g_post_mix, g_pre_ffn,
           g_post_ffn):
    bsz, _, d = x.shape
    depth = w_in.shape[0]
    meta = jnp.broadcast_to(meta_tokens.astype(x.dtype)[None], (bsz, N_META, d))
    h = jnp.concatenate([meta, x], axis=1)
    seq = h.shape[1]
    assert seq % ROW_TILE == 0 and ROW_TILE % HALO == 0
    h = h.reshape(bsz * seq, d)

    row = lambda v: v.reshape(1, -1)
    w_pack, w_ff = _pack_w_in(w_in)
    for layer in range(depth):
        bf_pad = jnp.pad(b_f[layer], (0, LANES - FOX_HEADS)).reshape(1, LANES)
        lam_init = 0.8 - 0.6 * math.exp(-0.3 * layer)

        proj, zf = _inproj(h, row(g_pre_mix[layer]), w_pack, w_ff, layer)
        proj3 = proj.reshape(bsz, seq, PACK_W)
        c4 = _forget_cumsum(zf.reshape(bsz, seq, LANES), bf_pad)
        y_fox = _fox_attention(proj3, c4).reshape(bsz * seq, FOX_W)
        y_diff = _diff_attention(proj3, row(lam_q1[layer]), row(lam_k1[layer]),
                                 row(lam_q2[layer]), row(lam_k2[layer]), row(g_sub[layer]),
                                 lam_init).reshape(bsz * seq, DIFF_W)
        y_conv = _conv_module(proj3, w_dw[layer], row(b_dw[layer]), row(ln_g[layer]),
                              row(ln_b[layer])).reshape(bsz * seq, CONV_CH)
        h = _outproj(y_fox, y_diff, y_conv, w_out[layer].astype(BF16), h, row(g_post_mix[layer]))
        h = _ffn(h, row(g_pre_ffn[layer]), w_gate[layer].astype(BF16), w_up[layer].astype(BF16),
                 w_ffn_conv[layer], w_down[layer].astype(BF16), row(g_post_ffn[layer]), seq)

    return h.reshape(bsz, seq, d)[:, N_META:]
```

```python
import functools
import math

import jax
import jax.numpy as jnp
from jax import lax
from jax.experimental import pallas as pl
from jax.experimental.pallas import tpu as pltpu

D_MODEL = 2048
N_META = 16
HEAD_DIM = 128
FOX_HEADS = 6
FOX_W = FOX_HEADS * HEAD_DIM
DIFF_HEADS = 4
DIFF_QK = 64
DIFF_V = 128
DIFF_W = DIFF_HEADS * DIFF_V
CONV_CH = 768
CONV_WIDTH = 31
D_FF = 5632
FFN_CONV = 3
EPS = 1e-6
SUBLN_EPS = 1e-5
FOX_SCALE = HEAD_DIM ** -0.5
DIFF_SCALE = DIFF_QK ** -0.5

O_FF = 3 * FOX_W
O_DQ = O_FF + FOX_HEADS

O_DV_END = O_DQ + 3 * DIFF_W
N_IN = O_DV_END + 2 * CONV_CH

PACK_W = 3 * DIFF_W + 3 * FOX_W + 2 * CONV_CH
P_FOX = 3 * DIFF_W
P_CONV = P_FOX + 3 * FOX_W
CONV_A_BLK, CONV_G_BLK = P_CONV // CONV_CH, P_CONV // CONV_CH + 1

LANES = 128
ROW_TILE = 688
HALO = 16
PROJ_TN = 768
FFN_TN = 512
ATT_T = 256
FOX_G = 6
DIFF_G = 4
LOG2E = math.log2(math.e)
MASK_VALUE = -1e30
VMEM_LIMIT = 56 * 1024 * 1024

F32 = jnp.float32
BF16 = jnp.bfloat16


def _rms_rows(x, g, eps):
    return (x * lax.rsqrt(jnp.mean(x * x, axis=-1, keepdims=True) + eps)) * g


def _sigmoid(x):
    return 1.0 / (1.0 + jnp.exp(-x))


def _dot_nt(a, b):
    return lax.dot_general(a, b, (((1,), (1,)), ((), ())), preferred_element_type=F32)


PACK_ROWS = 256


def _pack_kernel(w_ref, wp_ref, wff_ref):
    dq = w_ref[0, :, O_DQ:O_DQ + DIFF_W] * (DIFF_SCALE * LOG2E)
    fq = w_ref[0, :, 0:FOX_W] * (FOX_SCALE * LOG2E)
    wp_ref[0, :, 0:DIFF_W] = dq.astype(BF16)
    wp_ref[0, :, DIFF_W:P_FOX] = w_ref[0, :, O_DQ + DIFF_W:O_DV_END].astype(BF16)
    wp_ref[0, :, P_FOX:P_FOX + FOX_W] = fq.astype(BF16)
    wp_ref[0, :, P_FOX + FOX_W:P_CONV] = w_ref[0, :, FOX_W:O_FF].astype(BF16)
    wp_ref[0, :, P_CONV:PACK_W] = w_ref[0, :, O_DV_END:N_IN].astype(BF16)
    ff = w_ref[0, :, O_FF:O_FF + LANES]
    lane = lax.broadcasted_iota(jnp.int32, ff.shape, 1)
    wff_ref[0] = jnp.where(lane < FOX_HEADS, ff, 0.0).astype(BF16)


def _pack_w_in(w_in):
    depth, d, n_in = w_in.shape
    assert n_in == N_IN
    return pl.pallas_call(
        _pack_kernel,
        grid=(depth, d // PACK_ROWS),
        in_specs=[pl.BlockSpec((1, PACK_ROWS, n_in), lambda l, r: (l, r, 0))],
        out_specs=[
            pl.BlockSpec((1, PACK_ROWS, PACK_W), lambda l, r: (l, r, 0)),
            pl.BlockSpec((1, PACK_ROWS, LANES), lambda l, r: (l, r, 0)),
        ],
        out_shape=[
            jax.ShapeDtypeStruct((depth, d, PACK_W), BF16),
            jax.ShapeDtypeStruct((depth, d, LANES), BF16),
        ],
        compiler_params=pltpu.CompilerParams(dimension_semantics=("arbitrary", "arbitrary")),
        name="pack_w_in",
    )(w_in)


def _inproj_kernel(h_ref, g_ref, w_ref, wff_ref, proj_ref, zf_ref, hn_ref):
    @pl.when(pl.program_id(1) == 0)
    def _():
        hn = _rms_rows(h_ref[...], g_ref[...], EPS).astype(BF16)
        hn_ref[...] = hn
        zf_ref[...] = jnp.dot(hn, wff_ref[...], preferred_element_type=F32)

    proj_ref[...] = jnp.dot(hn_ref[...], w_ref[...], preferred_element_type=F32).astype(BF16)


def _inproj(h, g, w_pack, w_ff, layer):
    t = h.shape[0]
    grid = (t // ROW_TILE, PACK_W // PROJ_TN)
    return pl.pallas_call(
        _inproj_kernel,
        grid=grid,
        in_specs=[
            pl.BlockSpec((ROW_TILE, D_MODEL), lambda i, j: (i, 0)),
            pl.BlockSpec((1, D_MODEL), lambda i, j: (0, 0)),
            pl.BlockSpec((None, D_MODEL, PROJ_TN), lambda i, j: (layer, 0, j)),
            pl.BlockSpec((None, D_MODEL, LANES), lambda i, j: (layer, 0, 0)),
        ],
        out_specs=[
            pl.BlockSpec((ROW_TILE, PROJ_TN), lambda i, j: (i, j)),
            pl.BlockSpec((ROW_TILE, LANES), lambda i, j: (i, 0)),
        ],
        out_shape=[
            jax.ShapeDtypeStruct((t, PACK_W), BF16),
            jax.ShapeDtypeStruct((t, LANES), F32),
        ],
        scratch_shapes=[pltpu.VMEM((ROW_TILE, D_MODEL), BF16)],
        compiler_params=pltpu.CompilerParams(
            dimension_semantics=("arbitrary", "arbitrary"), vmem_limit_bytes=VMEM_LIMIT),
        name="inproj",
    )(h, g, w_pack, w_ff)


def _row_cumsum(x):
    row = lax.broadcasted_iota(jnp.int32, x.shape, 0)
    d = 1
    while d < x.shape[0]:
        x = x + jnp.where(row >= d, pltpu.roll(x, d, axis=0), 0.0)
        d *= 2
    return x


def _forget_kernel(zf_ref, bf_ref, c_ref, *, seq):
    carry = jnp.zeros((1, LANES), F32)
    for r0 in range(0, seq, LANES):
        rows = min(LANES, seq - r0)
        z = zf_ref[0, r0:r0 + rows, :] + bf_ref[...]
        lf = jnp.minimum(z, 0.0) - jnp.log(1.0 + jnp.exp(-jnp.abs(z)))
        cs = _row_cumsum(lf) + carry
        c_ref[0, r0:r0 + rows, :] = cs
        carry = cs[rows - 1:rows, :]


def _forget_cumsum(zf3, bf_pad):
    b, seq, _ = zf3.shape
    blk = pl.BlockSpec((1, seq, LANES), lambda i: (i, 0, 0))
    return pl.pallas_call(
        functools.partial(_forget_kernel, seq=seq),
        grid=(b,),
        in_specs=[blk, pl.BlockSpec((1, LANES), lambda i: (0, 0))],
        out_specs=blk,
        out_shape=jax.ShapeDtypeStruct((b, seq, LANES), F32),
        compiler_params=pltpu.CompilerParams(dimension_semantics=("arbitrary",)),
        name="forget_cumsum",
    )(zf3, bf_pad)


def _softmax_step(pieces, m, l):
    ss = []
    for s, bias, mask in pieces:
        if bias is not None:
            s = s + bias
        if mask is not None:
            s = jnp.where(mask, s, MASK_VALUE)
        ss.append(s)
    m_new = m
    for s in ss:
        m_new = jnp.maximum(m_new, jnp.max(s, axis=0, keepdims=True))
    alpha = jnp.exp2(m - m_new)
    l = alpha * l
    ps = []
    for s in ss:
        p = jnp.exp2(s - m_new)
        l = l + jnp.sum(p, axis=0, keepdims=True)
        ps.append(p.astype(BF16))
    return m_new, l, alpha, ps


def _pv(vt, p):
    return jnp.dot(vt, p, preferred_element_type=F32)


def _causal_attention(n_chains, seq, tq, n_maps, q_of, k_of, vt_of, bias_of, emit):
    first = seq % ATT_T
    assert 0 < first <= tq and (seq - first) % tq == 0 and ATT_T % tq == 0
    assert tq & (tq - 1) == 0
    n_qt = 1 + (seq - first) // tq
    lanes = tq * n_maps
    chains = range(n_chains)
    m0 = jnp.full((1, lanes), MASK_VALUE, F32)
    l0 = jnp.zeros((1, lanes), F32)

    def causal_mask(nk, q_off):
        key = lax.broadcasted_iota(jnp.int32, (nk, lanes), 0)
        qry = lax.broadcasted_iota(jnp.int32, (nk, lanes), 1) & (tq - 1)
        return key <= qry + q_off

    def scores(qs, k0, nk):
        return tuple(_dot_nt(k_of(g, k0, nk), qs[g]) for g in chains)

    qs = [q_of(g, 0) for g in chains]
    ss = scores(qs, 0, first)
    for g in chains:
        _, l, _, (p,) = _softmax_step([(ss[g], bias_of(g, 0, first), causal_mask(first, 0))],
                                      m0, l0)
        emit(g, 0, first, _pv(vt_of(g, 0, first), p) / l)

    def q_tile(i, _):
        done = (i - 1) * tq
        r0 = pl.multiple_of(first + done, HALO)
        n_before = done // ATT_T
        qs = [q_of(g, r0) for g in chains]
        s_first = scores(qs, 0, first)

        def body(j, carry):
            ss, states = carry
            nxt = scores(qs, pl.multiple_of(first + (j + 1) * ATT_T, HALO), ATT_T)
            out = []
            for g in chains:
                m, l, acc = states[g]
                m, l, alpha, (p,) = _softmax_step([(ss[g], bias_of(g, j + 1, ATT_T), None)], m, l)
                acc = alpha * acc + _pv(vt_of(g, j + 1, ATT_T), p)
                out.append((m, l, acc))
            return nxt, tuple(out)

        start = tuple((m0, l0, jnp.zeros((HEAD_DIM, lanes), F32)) for _ in chains)
        ss, states = lax.fori_loop(0, n_before, body, (scores(qs, first, ATT_T), start))
        diag = n_before + 1
        mask = causal_mask(ATT_T, done % ATT_T)
        for g in chains:
            m, l, acc = states[g]
            m, l, alpha, (p_d, p_f) = _softmax_step(
                [(ss[g], bias_of(g, diag, ATT_T), mask), (s_first[g], bias_of(g, 0, first), None)],
                m, l)
            acc = alpha * acc + _pv(vt_of(g, diag, ATT_T), p_d) + _pv(vt_of(g, 0, first), p_f)
            emit(g, r0, tq, acc / l)
        return 0

    lax.fori_loop(1, n_qt, q_tile, 0)


def _head_cols(g):
    return slice(g * HEAD_DIM, (g + 1) * HEAD_DIM)


def _key_tiles(seq):
    first = seq % ATT_T
    return [(0, first)] + [(r0, ATT_T) for r0 in range(first, seq, ATT_T)]


def _fill_vt(v_ref, vt_ref, heads, seq):
    for g in range(heads):
        for kt, (r0, nk) in enumerate(_key_tiles(seq)):
            v = v_ref[0, r0:r0 + nk, _head_cols(g)].astype(F32)
            if nk < LANES:
                v = jnp.concatenate([v, jnp.zeros((LANES - nk, HEAD_DIM), F32)], axis=0)
            vt_ref[g, kt, :, 0:v.shape[0]] = v.T.astype(BF16)


def _vt_of(vt_ref):
    def vt_of(g, kt, nk):
        return vt_ref[g, kt] if nk == ATT_T else vt_ref[g, kt, :, 0:nk]
    return vt_of


def _fox_kernel(q_ref, k_ref, v_ref, c_ref, o_ref, vt_ref, bias_ref, *, seq, heads):
    head0 = pl.program_id(1) * heads
    _fill_vt(v_ref, vt_ref, heads, seq)
    for kt, (r0, nk) in enumerate(_key_tiles(seq)):
        c = c_ref[0, r0:r0 + nk, :]
        lane = lax.broadcasted_iota(jnp.int32, c.shape, 1)
        for g in range(heads):
            col = jnp.sum(jnp.where(lane == head0 + g, c, 0.0), axis=1, keepdims=True)
            bias_ref[g, kt, 0:nk, :] = jnp.broadcast_to(col * (-LOG2E), (nk, LANES))

    def q_of(g, r0):
        return q_ref[0, pl.ds(r0, ATT_T), _head_cols(g)]

    def k_of(g, k0, nk):
        return k_ref[0, pl.ds(k0, nk), _head_cols(g)]

    def bias_of(g, kt, nk):
        b = bias_ref[g, kt] if nk == ATT_T else bias_ref[g, kt, 0:nk, :]
        return jnp.concatenate([b] * (ATT_T // LANES), axis=1)

    def emit(g, r0, n_rows, o):
        o_ref[0, pl.ds(r0, n_rows), _head_cols(g)] = o.T[0:n_rows].astype(BF16)

    _causal_attention(heads, seq, ATT_T, 1, q_of, k_of, _vt_of(vt_ref), bias_of, emit)


def _fox_attention(proj3, c3):
    b, seq, _ = proj3.shape
    n_kt = len(_key_tiles(seq))
    width = FOX_G * HEAD_DIM
    blk = (1, seq, width)
    base = P_FOX // width
    per = FOX_W // width
    return pl.pallas_call(
        functools.partial(_fox_kernel, seq=seq, heads=FOX_G),
        grid=(b, per),
        in_specs=[
            pl.BlockSpec(blk, lambda i, h: (i, 0, base + h)),
            pl.BlockSpec(blk, lambda i, h: (i, 0, base + per + h)),
            pl.BlockSpec(blk, lambda i, h: (i, 0, base + 2 * per + h)),
            pl.BlockSpec((1, seq, LANES), lambda i, h: (i, 0, 0)),
        ],
        out_specs=pl.BlockSpec(blk, lambda i, h: (i, 0, h)),
        out_shape=jax.ShapeDtypeStruct((b, seq, FOX_W), BF16),
        scratch_shapes=[
            pltpu.VMEM((FOX_G, n_kt, HEAD_DIM, ATT_T), BF16),
            pltpu.VMEM((FOX_G, n_kt, ATT_T, LANES), F32),
        ],
        compiler_params=pltpu.CompilerParams(
            dimension_semantics=("arbitrary", "arbitrary"), vmem_limit_bytes=VMEM_LIMIT),
        name="fox_attention",
    )(proj3, proj3, proj3, c3)


DIFF_TQ = ATT_T // 2


def _diff_kernel(q_ref, k_ref, v_ref, lq1_ref, lk1_ref, lq2_ref, lk2_ref, gsub_ref, o_ref, vt_ref,
                 *, seq, heads, lam_init):
    lam = (jnp.exp(jnp.sum(lq1_ref[...] * lk1_ref[...], axis=-1, keepdims=True))
           - jnp.exp(jnp.sum(lq2_ref[...] * lk2_ref[...], axis=-1, keepdims=True))
           + lam_init)
    _fill_vt(v_ref, vt_ref, heads, seq)

    def q_of(g, r0):
        q = q_ref[0, pl.ds(r0, DIFF_TQ), _head_cols(g)]
        lane = lax.broadcasted_iota(jnp.int32, q.shape, 1)
        zero = jnp.zeros_like(q)
        return jnp.concatenate([jnp.where(lane < DIFF_QK, q, zero),
                                jnp.where(lane >= DIFF_QK, q, zero)], axis=0)

    def k_of(g, k0, nk):
        return k_ref[0, pl.ds(k0, nk), _head_cols(g)]

    def emit(g, r0, n_rows, o):
        y = (o[:, 0:DIFF_TQ] - lam * o[:, DIFF_TQ:2 * DIFF_TQ]).T[0:n_rows]
        y = _rms_rows(y, gsub_ref[...], SUBLN_EPS) * (1.0 - lam_init)
        o_ref[0, pl.ds(r0, n_rows), _head_cols(g)] = y.astype(BF16)

    _causal_attention(heads, seq, DIFF_TQ, 2, q_of, k_of, _vt_of(vt_ref),
                      lambda g, kt, nk: None, emit)


def _diff_attention(proj3, lq1, lk1, lq2, lk2, g_sub, lam_init):
    b, seq, _ = proj3.shape
    n_kt = len(_key_tiles(seq))
    width = DIFF_G * HEAD_DIM
    blk = (1, seq, width)
    per = DIFF_W // width
    vec = pl.BlockSpec((1, DIFF_QK), lambda i, h: (0, 0))
    return pl.pallas_call(
        functools.partial(_diff_kernel, seq=seq, heads=DIFF_G, lam_init=lam_init),
        grid=(b, per),
        in_specs=[
            pl.BlockSpec(blk, lambda i, h: (i, 0, h)),
            pl.BlockSpec(blk, lambda i, h: (i, 0, per + h)),
            pl.BlockSpec(blk, lambda i, h: (i, 0, 2 * per + h)),
            vec, vec, vec, vec,
            pl.BlockSpec((1, DIFF_V), lambda i, h: (0, 0)),
        ],
        out_specs=pl.BlockSpec(blk, lambda i, h: (i, 0, h)),
        out_shape=jax.ShapeDtypeStruct((b, seq, DIFF_W), BF16),
        scratch_shapes=[pltpu.VMEM((DIFF_G, n_kt, HEAD_DIM, ATT_T), BF16)],
        compiler_params=pltpu.CompilerParams(
            dimension_semantics=("arbitrary", "arbitrary"), vmem_limit_bytes=VMEM_LIMIT),
        name="diff_attention",
    )(proj3, proj3, proj3, lq1, lk1, lq2, lk2, g_sub)


SUBLANES = 8
CONV_PAD = 32
CONV_CHUNK = 48


def _conv_kernel(a_ref, g_ref, w_ref, b_ref, lng_ref, lnb_ref, o_ref, u_ref, y_ref, *, seq):
    n_chunks = seq // CONV_CHUNK
    win = CONV_CHUNK + SUBLANES
    u_ref[0:CONV_PAD, :] = jnp.zeros((CONV_PAD, CONV_CH), F32)

    def glu(c, _):
        r0 = pl.multiple_of(c * CONV_CHUNK, CONV_CHUNK)
        a = a_ref[0, pl.ds(r0, CONV_CHUNK), :].astype(F32)
        g = g_ref[0, pl.ds(r0, CONV_CHUNK), :].astype(F32)
        u_ref[pl.ds(r0 + CONV_PAD, CONV_CHUNK), :] = a * _sigmoid(g)
        return 0

    lax.fori_loop(0, n_chunks, glu, 0)

    def chunk(c, _):
        r0 = pl.multiple_of(c * CONV_CHUNK, CONV_CHUNK)
        for lc in range(CONV_CH // LANES):
            cols = slice(lc * LANES, (lc + 1) * LANES)
            xs = [u_ref[pl.ds(r0 + CONV_PAD - SUBLANES * (a + 1), win), cols]
                  for a in range(-(-CONV_WIDTH // SUBLANES))]
            acc = None
            for r in range(SUBLANES):
                part = None
                for a, x in enumerate(xs):
                    s = SUBLANES * a + r
                    if s >= CONV_WIDTH:
                        continue
                    k = CONV_WIDTH - 1 - s
                    term = x * w_ref[k:k + 1, cols]
                    part = term if part is None else part + term
                if r:
                    part = pltpu.roll(part, r, axis=0)
                part = part[SUBLANES:, :]
                acc = part if acc is None else acc + part
            y_ref[:, cols] = acc
        u = y_ref[...] + b_ref[...]
        mu = jnp.mean(u, axis=-1, keepdims=True)
        uc = u - mu
        y = uc * lax.rsqrt(jnp.mean(uc * uc, axis=-1, keepdims=True) + EPS)
        y = y * lng_ref[...] + lnb_ref[...]
        o_ref[0, pl.ds(r0, CONV_CHUNK), :] = (y * _sigmoid(y)).astype(BF16)
        return 0

    lax.fori_loop(0, n_chunks, chunk, 0)


def _conv_module(proj3, w_dw, b_dw, ln_g, ln_b):
    b, seq, _ = proj3.shape
    assert seq % CONV_CHUNK == 0
    vec = pl.BlockSpec((1, CONV_CH), lambda i: (0, 0))
    return pl.pallas_call(
        functools.partial(_conv_kernel, seq=seq),
        grid=(b,),
        in_specs=[
            pl.BlockSpec((1, seq, CONV_CH), lambda i: (i, 0, CONV_A_BLK)),
            pl.BlockSpec((1, seq, CONV_CH), lambda i: (i, 0, CONV_G_BLK)),
            pl.BlockSpec((CONV_WIDTH, CONV_CH), lambda i: (0, 0)),
            vec, vec, vec,
        ],
        out_specs=pl.BlockSpec((1, seq, CONV_CH), lambda i: (i, 0, 0)),
        out_shape=jax.ShapeDtypeStruct((b, seq, CONV_CH), BF16),
        scratch_shapes=[
            pltpu.VMEM((CONV_PAD + seq, CONV_CH), F32),
            pltpu.VMEM((CONV_CHUNK, CONV_CH), F32),
        ],
        compiler_params=pltpu.CompilerParams(
            dimension_semantics=("arbitrary",), vmem_limit_bytes=VMEM_LIMIT),
        name="conv_module",
    )(proj3, proj3, w_dw, b_dw, ln_g, ln_b)


def _outproj_kernel(yf_ref, yd_ref, yc_ref, w_ref, h_ref, g_ref, o_ref):
    m = jnp.dot(yf_ref[...], w_ref[0:FOX_W, :], preferred_element_type=F32)
    m = m + jnp.dot(yd_ref[...], w_ref[FOX_W:FOX_W + DIFF_W, :], preferred_element_type=F32)
    m = m + jnp.dot(yc_ref[...], w_ref[FOX_W + DIFF_W:, :], preferred_element_type=F32)
    o_ref[...] = h_ref[...] + _rms_rows(m, g_ref[...], EPS)


def _outproj(y_fox, y_diff, y_conv, w_out, h, g):
    t = h.shape[0]
    rows = lambda w: pl.BlockSpec((ROW_TILE, w), lambda i: (i, 0))
    return pl.pallas_call(
        _outproj_kernel,
        grid=(t // ROW_TILE,),
        in_specs=[
            rows(FOX_W), rows(DIFF_W), rows(CONV_CH),
            pl.BlockSpec((D_MODEL, D_MODEL), lambda i: (0, 0)),
            rows(D_MODEL),
            pl.BlockSpec((1, D_MODEL), lambda i: (0, 0)),
        ],
        out_specs=rows(D_MODEL),
        out_shape=jax.ShapeDtypeStruct((t, D_MODEL), F32),
        compiler_params=pltpu.CompilerParams(
            dimension_semantics=("arbitrary",), vmem_limit_bytes=VMEM_LIMIT),
        name="outproj",
    )(y_fox, y_diff, y_conv, w_out, h, g)


def _ffn_kernel(h_ref, hp_ref, gpre_ref, wg_ref, wu_ref, wc_ref, wd_ref, gpost_ref, o_ref,
                hn_ref, gate_ref, acc_ref, *, tiles_per_seq):
    i, j = pl.program_id(0), pl.program_id(1)

    @pl.when(j == 0)
    def _():
        keep = jnp.where(i % tiles_per_seq == 0, 0.0, 1.0)
        hn_ref[0:HALO, :] = (_rms_rows(hp_ref[...], gpre_ref[...], EPS) * keep).astype(BF16)
        hn_ref[HALO:, :] = _rms_rows(h_ref[...], gpre_ref[...], EPS).astype(BF16)
        acc_ref[...] = jnp.zeros_like(acc_ref)

    gate_ref[...] = jnp.dot(hn_ref[...], wg_ref[...], preferred_element_type=F32)
    up = jnp.dot(hn_ref[HALO:, :], wu_ref[...], preferred_element_type=F32)
    conv = gate_ref[HALO:, :] * wc_ref[FFN_CONV - 1:FFN_CONV, :]
    for k in range(FFN_CONV - 1):
        back = FFN_CONV - 1 - k
        conv = conv + gate_ref[HALO - back:HALO - back + ROW_TILE, :] * wc_ref[k:k + 1, :]
    act = (conv * _sigmoid(conv) * up).astype(BF16)
    acc_ref[...] += jnp.dot(act, wd_ref[...], preferred_element_type=F32)

    @pl.when(j == pl.num_programs(1) - 1)
    def _():
        o_ref[...] = h_ref[...] + _rms_rows(acc_ref[...], gpost_ref[...], EPS)


def _ffn(h, g_pre, w_gate, w_up, w_conv, w_down, g_post, seq):
    t = h.shape[0]
    per = ROW_TILE // HALO
    return pl.pallas_call(
        functools.partial(_ffn_kernel, tiles_per_seq=seq // ROW_TILE),
        grid=(t // ROW_TILE, D_FF // FFN_TN),
        in_specs=[
            pl.BlockSpec((ROW_TILE, D_MODEL), lambda i, j: (i, 0)),
            pl.BlockSpec((HALO, D_MODEL), lambda i, j: (jnp.maximum(i * per - 1, 0), 0)),
            pl.BlockSpec((1, D_MODEL), lambda i, j: (0, 0)),
            pl.BlockSpec((D_MODEL, FFN_TN), lambda i, j: (0, j)),
            pl.BlockSpec((D_MODEL, FFN_TN), lambda i, j: (0, j)),
            pl.BlockSpec((FFN_CONV, FFN_TN), lambda i, j: (0, j)),
            pl.BlockSpec((FFN_TN, D_MODEL), lambda i, j: (j, 0)),
            pl.BlockSpec((1, D_MODEL), lambda i, j: (0, 0)),
        ],
        out_specs=pl.BlockSpec((ROW_TILE, D_MODEL), lambda i, j: (i, 0)),
        out_shape=jax.ShapeDtypeStruct((t, D_MODEL), F32),
        scratch_shapes=[
            pltpu.VMEM((HALO + ROW_TILE, D_MODEL), BF16),
            pltpu.VMEM((HALO + ROW_TILE, FFN_TN), F32),
            pltpu.VMEM((ROW_TILE, D_MODEL), F32),
        ],
        compiler_params=pltpu.CompilerParams(
            dimension_semantics=("arbitrary", "arbitrary"), vmem_limit_bytes=VMEM_LIMIT),
        name="ffn",
    )(h, h, g_pre, w_gate, w_up, w_conv, w_down, g_post)


def kernel(x, meta_tokens, w_in, b_f, lam_q1, lam_k1, lam_q2, lam_k2, g_sub, w_dw, b_dw, ln_g,
           ln_b, w_out, w_gate, w_up, w_ffn_conv, w_down, g_pre_mix, g_post_mix, g_pre_ffn,
           g_post_ffn):
    bsz, _, d = x.shape
    depth = w_in.shape[0]
    meta = jnp.broadcast_to(meta_tokens.astype(x.dtype)[None], (bsz, N_META, d))
    h = jnp.concatenate([meta, x], axis=1)
    seq = h.shape[1]
    assert seq % ROW_TILE == 0 and ROW_TILE % HALO == 0
    h = h.reshape(bsz * seq, d)

    row = lambda v: v.reshape(1, -1)
    w_pack, w_ff = _pack_w_in(w_in)
    for layer in range(depth):
        bf_pad = jnp.pad(b_f[layer], (0, LANES - FOX_HEADS)).reshape(1, LANES)
        lam_init = 0.8 - 0.6 * math.exp(-0.3 * layer)

        proj, zf = _inproj(h, row(g_pre_mix[layer]), w_pack, w_ff, layer)
        proj3 = proj.reshape(bsz, seq, PACK_W)
        c4 = _forget_cumsum(zf.reshape(bsz, seq, LANES), bf_pad)
        y_fox = _fox_attention(proj3, c4).reshape(bsz * seq, FOX_W)
        y_diff = _diff_attention(proj3, row(lam_q1[layer]), row(lam_k1[layer]),
                                 row(lam_q2[layer]), row(lam_k2[layer]), row(g_sub[layer]),
                                 lam_init).reshape(bsz * seq, DIFF_W)
        y_conv = _conv_module(proj3, w_dw[layer], row(b_dw[layer]), row(ln_g[layer]),
                              row(ln_b[layer])).reshape(bsz * seq, CONV_CH)
        h = _outproj(y_fox, y_diff, y_conv, w_out[layer].astype(BF16), h, row(g_post_mix[layer]))
        h = _ffn(h, row(g_pre_ffn[layer]), w_gate[layer].astype(BF16), w_up[layer].astype(BF16),
                 w_ffn_conv[layer], w_down[layer].astype(BF16), row(g_post_ffn[layer]), seq)

    return h.reshape(bsz, seq, d)[:, N_META:]
```

```python
import functools
import math

import jax
import jax.numpy as jnp
from jax import lax
from jax.experimental import pallas as pl
from jax.experimental.pallas import tpu as pltpu

D_MODEL = 2048
N_META = 16
HEAD_DIM = 128
FOX_HEADS = 6
FOX_W = FOX_HEADS * HEAD_DIM
DIFF_HEADS = 4
DIFF_QK = 64
DIFF_V = 128
DIFF_W = DIFF_HEADS * DIFF_V
CONV_CH = 768
CONV_WIDTH = 31
D_FF = 5632
FFN_CONV = 3
EPS = 1e-6
SUBLN_EPS = 1e-5
FOX_SCALE = HEAD_DIM ** -0.5
DIFF_SCALE = DIFF_QK ** -0.5

O_FF = 3 * FOX_W
O_DQ = O_FF + FOX_HEADS

O_DV_END = O_DQ + 3 * DIFF_W
N_IN = O_DV_END + 2 * CONV_CH

PACK_W = 2 * CONV_CH + 3 * DIFF_W + 3 * FOX_W
PK_DIFF = 2 * CONV_CH
PK_FOX = PK_DIFF + 3 * DIFF_W
PROJ_W = PACK_W - 2 * CONV_CH
P_FOX = 3 * DIFF_W

LANES = 128
MXU_N = 256
ROW_TILE = 688
HALO = 16
PROJ_TN = 768
FFN_TN = 512
ATT_T = 256
FOX_G = 6
DIFF_G = 4
LOG2E = math.log2(math.e)
MASK_VALUE = -1e30
VMEM_LIMIT = 56 * 1024 * 1024

F32 = jnp.float32
BF16 = jnp.bfloat16


def _rms_rows(x, g, eps):
    return (x * lax.rsqrt(jnp.mean(x * x, axis=-1, keepdims=True) + eps)) * g


def _sigmoid(x):
    return 1.0 / (1.0 + jnp.exp(-x))


def _dot_nt(a, b):
    return lax.dot_general(a, b, (((1,), (1,)), ((), ())), preferred_element_type=F32)


PACK_ROWS = 256


def _pack_kernel(w_ref, wp_ref, wff_ref):
    dq = w_ref[0, :, O_DQ:O_DQ + DIFF_W] * (DIFF_SCALE * LOG2E)
    fq = w_ref[0, :, 0:FOX_W] * (FOX_SCALE * LOG2E)
    wp_ref[0, :, 0:PK_DIFF] = w_ref[0, :, O_DV_END:N_IN].astype(BF16)
    wp_ref[0, :, PK_DIFF:PK_DIFF + DIFF_W] = dq.astype(BF16)
    wp_ref[0, :, PK_DIFF + DIFF_W:PK_FOX] = w_ref[0, :, O_DQ + DIFF_W:O_DV_END].astype(BF16)
    wp_ref[0, :, PK_FOX:PK_FOX + FOX_W] = fq.astype(BF16)
    wp_ref[0, :, PK_FOX + FOX_W:PACK_W] = w_ref[0, :, FOX_W:O_FF].astype(BF16)
    ff = w_ref[0, :, O_FF:O_FF + LANES]
    lane = lax.broadcasted_iota(jnp.int32, ff.shape, 1)
    wff_ref[0] = jnp.where(lane < FOX_HEADS, ff, 0.0).astype(BF16)


def _pack_w_in(w_in):
    depth, d, n_in = w_in.shape
    assert n_in == N_IN
    return pl.pallas_call(
        _pack_kernel,
        grid=(depth, d // PACK_ROWS),
        in_specs=[pl.BlockSpec((1, PACK_ROWS, n_in), lambda l, r: (l, r, 0))],
        out_specs=[
            pl.BlockSpec((1, PACK_ROWS, PACK_W), lambda l, r: (l, r, 0)),
            pl.BlockSpec((1, PACK_ROWS, LANES), lambda l, r: (l, r, 0)),
        ],
        out_shape=[
            jax.ShapeDtypeStruct((depth, d, PACK_W), BF16),
            jax.ShapeDtypeStruct((depth, d, LANES), BF16),
        ],
        compiler_params=pltpu.CompilerParams(dimension_semantics=("arbitrary", "arbitrary")),
        name="pack_w_in",
    )(w_in)


SUBLANES = 8
CONV_PAD = 32
CONV_CHUNK = 16
CONV_TAP_GROUPS = -(-CONV_WIDTH // SUBLANES)
PROJ_STEPS = PACK_W // PROJ_TN


def _conv_chunk(u_ref, slot, r0, w_ref, b_ref, lng_ref, lnb_ref):
    win = CONV_CHUNK + SUBLANES
    cols_out = []
    for lc in range(CONV_CH // LANES):
        cols = slice(lc * LANES, (lc + 1) * LANES)
        xs = [u_ref[slot, pl.ds(r0 + CONV_PAD - SUBLANES * (a + 1), win), cols]
              for a in range(CONV_TAP_GROUPS)]
        acc = None
        for r in range(SUBLANES):
            part = None
            for a, x in enumerate(xs):
                s = SUBLANES * a + r
                if s >= CONV_WIDTH:
                    continue
                k = CONV_WIDTH - 1 - s
                term = x * w_ref[k:k + 1, cols]
                part = term if part is None else part + term
            if r:
                part = pltpu.roll(part, r, axis=0)
            part = part[SUBLANES:, :]
            acc = part if acc is None else acc + part
        cols_out.append(acc)
    u = jnp.concatenate(cols_out, axis=1) + b_ref[...]
    mu = jnp.mean(u, axis=-1, keepdims=True)
    uc = u - mu
    y = uc * lax.rsqrt(jnp.mean(uc * uc, axis=-1, keepdims=True) + EPS)
    y = y * lng_ref[...] + lnb_ref[...]
    return (y * _sigmoid(y)).astype(BF16)


def _inproj_kernel(h_ref, g_ref, w_ref, wff_ref, wdw_ref, bdw_ref, lng_ref, lnb_ref,
                   proj_ref, zf_ref, yc_ref, hn_ref, a_ref, u_ref, *, tiles_per_seq):
    i, j = pl.program_id(0), pl.program_id(1)
    slot = i % 2

    @pl.when(jnp.logical_and(i == 0, j == 0))
    def _():
        u_ref[...] = jnp.zeros_like(u_ref)

    @pl.when(j == 0)
    def _():
        hn = _rms_rows(h_ref[...], g_ref[...], EPS).astype(BF16)
        hn_ref[...] = hn
        zf_ref[...] = jnp.dot(hn, wff_ref[...], preferred_element_type=F32)

    chunks = -(-(ROW_TILE // CONV_CHUNK) // PROJ_STEPS)
    pieces = PROJ_TN // MXU_N
    done = 0
    for n in range(pieces):
        cols = slice(n * MXU_N, (n + 1) * MXU_N)
        part = jnp.dot(hn_ref[...], w_ref[:, cols], preferred_element_type=F32)
        proj_ref[:, cols] = part.astype(BF16)
        upto = chunks * (n + 1) // pieces
        for q in range(done, upto):
            c = jnp.minimum(j * chunks + q, ROW_TILE // CONV_CHUNK - 1)
            r0 = pl.multiple_of(c * CONV_CHUNK, CONV_CHUNK)
            yc_ref[pl.ds(r0, CONV_CHUNK), :] = _conv_chunk(
                u_ref, 1 - slot, r0, wdw_ref, bdw_ref, lng_ref, lnb_ref)
        done = upto

    @pl.when(j == 0)
    def _():
        a_ref[...] = proj_ref[...]

    @pl.when(j == 1)
    def _():
        keep = jnp.where(i % tiles_per_seq == 0, 0.0, 1.0)
        u_ref[slot, 0:CONV_PAD, :] = u_ref[1 - slot, ROW_TILE:ROW_TILE + CONV_PAD, :] * keep
        u_ref[slot, CONV_PAD:, :] = a_ref[...].astype(F32) * _sigmoid(proj_ref[...].astype(F32))


def _inproj(h, g, w_pack, w_ff, w_dw, b_dw, ln_g, ln_b, layer, seq):
    t = h.shape[0]
    n_tiles = t // ROW_TILE
    last = n_tiles - 1
    conv_steps = 2 * CONV_CH // PROJ_TN
    vec = pl.BlockSpec((1, CONV_CH), lambda i, j: (0, 0))
    rows = lambda i: jnp.minimum(i, last)
    return pl.pallas_call(
        functools.partial(_inproj_kernel, tiles_per_seq=seq // ROW_TILE),
        grid=(n_tiles + 1, PROJ_STEPS),
        in_specs=[
            pl.BlockSpec((ROW_TILE, D_MODEL), lambda i, j: (rows(i), 0)),
            pl.BlockSpec((1, D_MODEL), lambda i, j: (0, 0)),
            pl.BlockSpec((None, D_MODEL, PROJ_TN), lambda i, j: (layer, 0, j)),
            pl.BlockSpec((None, D_MODEL, LANES), lambda i, j: (layer, 0, 0)),
            pl.BlockSpec((None, CONV_WIDTH, CONV_CH), lambda i, j: (layer, 0, 0)),
            vec, vec, vec,
        ],
        out_specs=[
            pl.BlockSpec((ROW_TILE, PROJ_TN), lambda i, j: (i, jnp.maximum(j - conv_steps, 0))),
            pl.BlockSpec((ROW_TILE, LANES), lambda i, j: (i, 0)),
            pl.BlockSpec((ROW_TILE, CONV_CH), lambda i, j: (jnp.maximum(i - 1, 0), 0)),
        ],
        out_shape=[
            jax.ShapeDtypeStruct((t + ROW_TILE, PROJ_W), BF16),
            jax.ShapeDtypeStruct((t + ROW_TILE, LANES), F32),
            jax.ShapeDtypeStruct((t, CONV_CH), BF16),
        ],
        scratch_shapes=[
            pltpu.VMEM((ROW_TILE, D_MODEL), BF16),
            pltpu.VMEM((ROW_TILE, CONV_CH), BF16),
            pltpu.VMEM((2, CONV_PAD + ROW_TILE, CONV_CH), F32),
        ],
        compiler_params=pltpu.CompilerParams(
            dimension_semantics=("arbitrary", "arbitrary"), vmem_limit_bytes=VMEM_LIMIT),
        name="inproj",
    )(h, g, w_pack, w_ff, w_dw, b_dw, ln_g, ln_b)


def _row_cumsum(x):
    row = lax.broadcasted_iota(jnp.int32, x.shape, 0)
    d = 1
    while d < x.shape[0]:
        x = x + jnp.where(row >= d, pltpu.roll(x, d, axis=0), 0.0)
        d *= 2
    return x


def _forget_kernel(zf_ref, bf_ref, c_ref, *, seq):
    carry = jnp.zeros((1, LANES), F32)
    for r0 in range(0, seq, LANES):
        rows = min(LANES, seq - r0)
        z = zf_ref[r0:r0 + rows, :] + bf_ref[...]
        lf = jnp.minimum(z, 0.0) - jnp.log(1.0 + jnp.exp(-jnp.abs(z)))
        cs = _row_cumsum(lf) + carry
        c_ref[r0:r0 + rows, :] = cs
        carry = cs[rows - 1:rows, :]


def _forget_cumsum(zf, bf_pad, b, seq):
    blk = pl.BlockSpec((seq, LANES), lambda i: (i, 0))
    return pl.pallas_call(
        functools.partial(_forget_kernel, seq=seq),
        grid=(b,),
        in_specs=[blk, pl.BlockSpec((1, LANES), lambda i: (0, 0))],
        out_specs=blk,
        out_shape=jax.ShapeDtypeStruct((b * seq, LANES), F32),
        compiler_params=pltpu.CompilerParams(dimension_semantics=("arbitrary",)),
        name="forget_cumsum",
    )(zf, bf_pad)


def _softmax_step(pieces, m, l):
    ss = []
    for s, bias, mask in pieces:
        if bias is not None:
            s = s + bias
        if mask is not None:
            s = jnp.where(mask, s, MASK_VALUE)
        ss.append(s)
    m_new = m
    for s in ss:
        m_new = jnp.maximum(m_new, jnp.max(s, axis=0, keepdims=True))
    alpha = jnp.exp2(m - m_new)
    l = alpha * l
    ps = []
    for s in ss:
        p = jnp.exp2(s - m_new)
        l = l + jnp.sum(p, axis=0, keepdims=True)
        ps.append(p.astype(BF16))
    return m_new, l, alpha, ps


def _pv(vt, p):
    return jnp.dot(vt, p, preferred_element_type=F32)


def _causal_attention(n_chains, seq, tq, n_maps, q_of, k_of, vt_of, bias_of, emit):
    first = seq % ATT_T
    assert 0 < first <= tq and (seq - first) % tq == 0 and ATT_T % tq == 0
    assert tq & (tq - 1) == 0
    n_qt = 1 + (seq - first) // tq
    lanes = tq * n_maps
    chains = range(n_chains)
    m0 = jnp.full((1, lanes), MASK_VALUE, F32)
    l0 = jnp.zeros((1, lanes), F32)

    def causal_mask(nk, q_off):
        key = lax.broadcasted_iota(jnp.int32, (nk, lanes), 0)
        qry = lax.broadcasted_iota(jnp.int32, (nk, lanes), 1) & (tq - 1)
        return key <= qry + q_off

    def scores(qs, k0, nk):
        return tuple(_dot_nt(k_of(g, k0, nk), qs[g]) for g in chains)

    qs = [q_of(g, 0) for g in chains]
    ss = scores(qs, 0, first)
    for g in chains:
        _, l, _, (p,) = _softmax_step([(ss[g], bias_of(g, 0, first), causal_mask(first, 0))],
                                      m0, l0)
        emit(g, 0, first, _pv(vt_of(g, 0, first), p) / l)

    def q_tile(i, _):
        done = (i - 1) * tq
        r0 = pl.multiple_of(first + done, HALO)
        n_before = done // ATT_T
        qs = [q_of(g, r0) for g in chains]
        s_first = scores(qs, 0, first)

        def body(j, carry):
            ss, states = carry
            nxt = scores(qs, pl.multiple_of(first + (j + 1) * ATT_T, HALO), ATT_T)
            out = []
            for g in chains:
                m, l, acc = states[g]
                m, l, alpha, (p,) = _softmax_step([(ss[g], bias_of(g, j + 1, ATT_T), None)], m, l)
                acc = alpha * acc + _pv(vt_of(g, j + 1, ATT_T), p)
                out.append((m, l, acc))
            return nxt, tuple(out)

        start = tuple((m0, l0, jnp.zeros((HEAD_DIM, lanes), F32)) for _ in chains)
        ss, states = lax.fori_loop(0, n_before, body, (scores(qs, first, ATT_T), start))
        diag = n_before + 1
        mask = causal_mask(ATT_T, done % ATT_T)
        for g in chains:
            m, l, acc = states[g]
            m, l, alpha, (p_d, p_f) = _softmax_step(
                [(ss[g], bias_of(g, diag, ATT_T), mask), (s_first[g], bias_of(g, 0, first), None)],
                m, l)
            acc = alpha * acc + _pv(vt_of(g, diag, ATT_T), p_d) + _pv(vt_of(g, 0, first), p_f)
            emit(g, r0, tq, acc / l)
        return 0

    lax.fori_loop(1, n_qt, q_tile, 0)


def _head_cols(g):
    return slice(g * HEAD_DIM, (g + 1) * HEAD_DIM)


def _key_tiles(seq):
    first = seq % ATT_T
    return [(0, first)] + [(r0, ATT_T) for r0 in range(first, seq, ATT_T)]


def _fill_vt(v_ref, vt_ref, heads, seq):
    for g in range(heads):
        for kt, (r0, nk) in enumerate(_key_tiles(seq)):
            v = v_ref[r0:r0 + nk, _head_cols(g)].astype(F32)
            if nk < LANES:
                v = jnp.concatenate([v, jnp.zeros((LANES - nk, HEAD_DIM), F32)], axis=0)
            vt_ref[g, kt, :, 0:v.shape[0]] = v.T.astype(BF16)


def _vt_of(vt_ref):
    def vt_of(g, kt, nk):
        return vt_ref[g, kt] if nk == ATT_T else vt_ref[g, kt, :, 0:nk]
    return vt_of


def _fox_kernel(q_ref, k_ref, v_ref, c_ref, o_ref, vt_ref, bias_ref, *, seq, heads):
    head0 = pl.program_id(1) * heads
    _fill_vt(v_ref, vt_ref, heads, seq)
    for kt, (r0, nk) in enumerate(_key_tiles(seq)):
        c = c_ref[r0:r0 + nk, :]
        lane = lax.broadcasted_iota(jnp.int32, c.shape, 1)
        for g in range(heads):
            col = jnp.sum(jnp.where(lane == head0 + g, c, 0.0), axis=1, keepdims=True)
            bias_ref[g, kt, 0:nk, :] = jnp.broadcast_to(col * (-LOG2E), (nk, LANES))

    def q_of(g, r0):
        return q_ref[pl.ds(r0, ATT_T), _head_cols(g)]

    def k_of(g, k0, nk):
        return k_ref[pl.ds(k0, nk), _head_cols(g)]

    def bias_of(g, kt, nk):
        b = bias_ref[g, kt] if nk == ATT_T else bias_ref[g, kt, 0:nk, :]
        return jnp.concatenate([b] * (ATT_T // LANES), axis=1)

    def emit(g, r0, n_rows, o):
        o_ref[pl.ds(r0, n_rows), _head_cols(g)] = o.T[0:n_rows].astype(BF16)

    _causal_attention(heads, seq, ATT_T, 1, q_of, k_of, _vt_of(vt_ref), bias_of, emit)


def _fox_attention(proj, c, b, seq):
    n_kt = len(_key_tiles(seq))
    width = FOX_G * HEAD_DIM
    blk = (seq, width)
    base = P_FOX // width
    per = FOX_W // width
    return pl.pallas_call(
        functools.partial(_fox_kernel, seq=seq, heads=FOX_G),
        grid=(b, per),
        in_specs=[
            pl.BlockSpec(blk, lambda i, h: (i, base + h)),
            pl.BlockSpec(blk, lambda i, h: (i, base + per + h)),
            pl.BlockSpec(blk, lambda i, h: (i, base + 2 * per + h)),
            pl.BlockSpec((seq, LANES), lambda i, h: (i, 0)),
        ],
        out_specs=pl.BlockSpec(blk, lambda i, h: (i, h)),
        out_shape=jax.ShapeDtypeStruct((b * seq, FOX_W), BF16),
        scratch_shapes=[
            pltpu.VMEM((FOX_G, n_kt, HEAD_DIM, ATT_T), BF16),
            pltpu.VMEM((FOX_G, n_kt, ATT_T, LANES), F32),
        ],
        compiler_params=pltpu.CompilerParams(
            dimension_semantics=("arbitrary", "arbitrary"), vmem_limit_bytes=VMEM_LIMIT),
        name="fox_attention",
    )(proj, proj, proj, c)


DIFF_TQ = ATT_T // 2


def _diff_kernel(q_ref, k_ref, v_ref, lq1_ref, lk1_ref, lq2_ref, lk2_ref, gsub_ref, o_ref, vt_ref,
                 *, seq, heads, lam_init):
    lam = (jnp.exp(jnp.sum(lq1_ref[...] * lk1_ref[...], axis=-1, keepdims=True))
           - jnp.exp(jnp.sum(lq2_ref[...] * lk2_ref[...], axis=-1, keepdims=True))
           + lam_init)
    _fill_vt(v_ref, vt_ref, heads, seq)

    def q_of(g, r0):
        q = q_ref[pl.ds(r0, DIFF_TQ), _head_cols(g)]
        lane = lax.broadcasted_iota(jnp.int32, q.shape, 1)
        zero = jnp.zeros_like(q)
        return jnp.concatenate([jnp.where(lane < DIFF_QK, q, zero),
                                jnp.where(lane >= DIFF_QK, q, zero)], axis=0)

    def k_of(g, k0, nk):
        return k_ref[pl.ds(k0, nk), _head_cols(g)]

    def emit(g, r0, n_rows, o):
        y = (o[:, 0:DIFF_TQ] - lam * o[:, DIFF_TQ:2 * DIFF_TQ]).T[0:n_rows]
        y = _rms_rows(y, gsub_ref[...], SUBLN_EPS) * (1.0 - lam_init)
        o_ref[pl.ds(r0, n_rows), _head_cols(g)] = y.astype(BF16)

    _causal_attention(heads, seq, DIFF_TQ, 2, q_of, k_of, _vt_of(vt_ref),
                      lambda g, kt, nk: None, emit)


def _diff_attention(proj, lq1, lk1, lq2, lk2, g_sub, lam_init, b, seq):
    n_kt = len(_key_tiles(seq))
    width = DIFF_G * HEAD_DIM
    blk = (seq, width)
    per = DIFF_W // width
    vec = pl.BlockSpec((1, DIFF_QK), lambda i, h: (0, 0))
    return pl.pallas_call(
        functools.partial(_diff_kernel, seq=seq, heads=DIFF_G, lam_init=lam_init),
        grid=(b, per),
        in_specs=[
            pl.BlockSpec(blk, lambda i, h: (i, h)),
            pl.BlockSpec(blk, lambda i, h: (i, per + h)),
            pl.BlockSpec(blk, lambda i, h: (i, 2 * per + h)),
            vec, vec, vec, vec,
            pl.BlockSpec((1, DIFF_V), lambda i, h: (0, 0)),
        ],
        out_specs=pl.BlockSpec(blk, lambda i, h: (i, h)),
        out_shape=jax.ShapeDtypeStruct((b * seq, DIFF_W), BF16),
        scratch_shapes=[pltpu.VMEM((DIFF_G, n_kt, HEAD_DIM, ATT_T), BF16)],
        compiler_params=pltpu.CompilerParams(
            dimension_semantics=("arbitrary", "arbitrary"), vmem_limit_bytes=VMEM_LIMIT),
        name="diff_attention",
    )(proj, proj, proj, lq1, lk1, lq2, lk2, g_sub)


def _outproj_kernel(yf_ref, yd_ref, yc_ref, w_ref, h_ref, g_ref, o_ref):
    m = jnp.dot(yf_ref[...], w_ref[0:FOX_W, :], preferred_element_type=F32)
    m = m + jnp.dot(yd_ref[...], w_ref[FOX_W:FOX_W + DIFF_W, :], preferred_element_type=F32)
    m = m + jnp.dot(yc_ref[...], w_ref[FOX_W + DIFF_W:, :], preferred_element_type=F32)
    o_ref[...] = h_ref[...] + _rms_rows(m, g_ref[...], EPS)


def _outproj(y_fox, y_diff, y_conv, w_out, h, g):
    t = h.shape[0]
    rows = lambda w: pl.BlockSpec((ROW_TILE, w), lambda i: (i, 0))
    return pl.pallas_call(
        _outproj_kernel,
        grid=(t // ROW_TILE,),
        in_specs=[
            rows(FOX_W), rows(DIFF_W), rows(CONV_CH),
            pl.BlockSpec((D_MODEL, D_MODEL), lambda i: (0, 0)),
            rows(D_MODEL),
            pl.BlockSpec((1, D_MODEL), lambda i: (0, 0)),
        ],
        out_specs=rows(D_MODEL),
        out_shape=jax.ShapeDtypeStruct((t, D_MODEL), F32),
        compiler_params=pltpu.CompilerParams(
            dimension_semantics=("arbitrary",), vmem_limit_bytes=VMEM_LIMIT),
        name="outproj",
    )(y_fox, y_diff, y_conv, w_out, h, g)


def _ffn_kernel(h_ref, hp_ref, gpre_ref, wg_ref, wu_ref, wc_ref, wd_ref, gpost_ref, o_ref,
                hn_ref, gate_ref, acc_ref, *, tiles_per_seq):
    i, j = pl.program_id(0), pl.program_id(1)

    @pl.when(j == 0)
    def _():
        keep = jnp.where(i % tiles_per_seq == 0, 0.0, 1.0)
        hn_ref[0:HALO, :] = (_rms_rows(hp_ref[...], gpre_ref[...], EPS) * keep).astype(BF16)
        hn_ref[HALO:, :] = _rms_rows(h_ref[...], gpre_ref[...], EPS).astype(BF16)
        acc_ref[...] = jnp.zeros_like(acc_ref)

    gate_ref[...] = jnp.dot(hn_ref[...], wg_ref[...], preferred_element_type=F32)
    up = jnp.dot(hn_ref[HALO:, :], wu_ref[...], preferred_element_type=F32)
    conv = gate_ref[HALO:, :] * wc_ref[FFN_CONV - 1:FFN_CONV, :]
    for k in range(FFN_CONV - 1):
        back = FFN_CONV - 1 - k
        conv = conv + gate_ref[HALO - back:HALO - back + ROW_TILE, :] * wc_ref[k:k + 1, :]
    act = (conv * _sigmoid(conv) * up).astype(BF16)
    acc_ref[...] += jnp.dot(act, wd_ref[...], preferred_element_type=F32)

    @pl.when(j == pl.num_programs(1) - 1)
    def _():
        o_ref[...] = h_ref[...] + _rms_rows(acc_ref[...], gpost_ref[...], EPS)


def _ffn(h, g_pre, w_gate, w_up, w_conv, w_down, g_post, seq):
    t = h.shape[0]
    per = ROW_TILE // HALO
    return pl.pallas_call(
        functools.partial(_ffn_kernel, tiles_per_seq=seq // ROW_TILE),
        grid=(t // ROW_TILE, D_FF // FFN_TN),
        in_specs=[
            pl.BlockSpec((ROW_TILE, D_MODEL), lambda i, j: (i, 0)),
            pl.BlockSpec((HALO, D_MODEL), lambda i, j: (jnp.maximum(i * per - 1, 0), 0)),
            pl.BlockSpec((1, D_MODEL), lambda i, j: (0, 0)),
            pl.BlockSpec((D_MODEL, FFN_TN), lambda i, j: (0, j)),
            pl.BlockSpec((D_MODEL, FFN_TN), lambda i, j: (0, j)),
            pl.BlockSpec((FFN_CONV, FFN_TN), lambda i, j: (0, j)),
            pl.BlockSpec((FFN_TN, D_MODEL), lambda i, j: (j, 0)),
            pl.BlockSpec((1, D_MODEL), lambda i, j: (0, 0)),
        ],
        out_specs=pl.BlockSpec((ROW_TILE, D_MODEL), lambda i, j: (i, 0)),
        out_shape=jax.ShapeDtypeStruct((t, D_MODEL), F32),
        scratch_shapes=[
            pltpu.VMEM((HALO + ROW_TILE, D_MODEL), BF16),
            pltpu.VMEM((HALO + ROW_TILE, FFN_TN), F32),
            pltpu.VMEM((ROW_TILE, D_MODEL), F32),
        ],
        compiler_params=pltpu.CompilerParams(
            dimension_semantics=("arbitrary", "arbitrary"), vmem_limit_bytes=VMEM_LIMIT),
        name="ffn",
    )(h, h, g_pre, w_gate, w_up, w_conv, w_down, g_post)


def kernel(x, meta_tokens, w_in, b_f, lam_q1, lam_k1, lam_q2, lam_k2, g_sub, w_dw, b_dw, ln_g,
           ln_b, w_out, w_gate, w_up, w_ffn_conv, w_down, g_pre_mix, g_post_mix, g_pre_ffn,
           g_post_ffn):
    bsz, _, d = x.shape
    depth = w_in.shape[0]
    meta = jnp.broadcast_to(meta_tokens.astype(x.dtype)[None], (bsz, N_META, d))
    h = jnp.concatenate([meta, x], axis=1)
    seq = h.shape[1]
    assert seq % ROW_TILE == 0 and ROW_TILE % HALO == 0
    h = h.reshape(bsz * seq, d)

    row = lambda v: v.reshape(1, -1)
    w_pack, w_ff = _pack_w_in(w_in)
    for layer in range(depth):
        bf_pad = jnp.pad(b_f[layer], (0, LANES - FOX_HEADS)).reshape(1, LANES)
        lam_init = 0.8 - 0.6 * math.exp(-0.3 * layer)

        proj, zf, y_conv = _inproj(h, row(g_pre_mix[layer]), w_pack, w_ff, w_dw,
                                   row(b_dw[layer]), row(ln_g[layer]), row(ln_b[layer]), layer, seq)
        c = _forget_cumsum(zf, bf_pad, bsz, seq)
        y_fox = _fox_attention(proj, c, bsz, seq)
        y_diff = _diff_attention(proj, row(lam_q1[layer]), row(lam_k1[layer]), row(lam_q2[layer]),
                                 row(lam_k2[layer]), row(g_sub[layer]), lam_init, bsz, seq)
        h = _outproj(y_fox, y_diff, y_conv, w_out[layer].astype(BF16), h, row(g_post_mix[layer]))
        h = _ffn(h, row(g_pre_ffn[layer]), w_gate[layer].astype(BF16), w_up[layer].astype(BF16),
                 w_ffn_conv[layer], w_down[layer].astype(BF16), row(g_post_ffn[layer]), seq)

    return h.reshape(bsz, seq, d)[:, N_META:]
```

```python
import functools
import math

import jax
import jax.numpy as jnp
from jax import lax
from jax.experimental import pallas as pl
from jax.experimental.pallas import tpu as pltpu

D_MODEL = 2048
N_META = 16
HEAD_DIM = 128
FOX_HEADS = 6
FOX_W = FOX_HEADS * HEAD_DIM
DIFF_HEADS = 4
DIFF_QK = 64
DIFF_V = 128
DIFF_W = DIFF_HEADS * DIFF_V
CONV_CH = 768
CONV_WIDTH = 31
D_FF = 5632
FFN_CONV = 3
EPS = 1e-6
SUBLN_EPS = 1e-5
FOX_SCALE = HEAD_DIM ** -0.5
DIFF_SCALE = DIFF_QK ** -0.5

O_FF = 3 * FOX_W
O_DQ = O_FF + FOX_HEADS

O_DV_END = O_DQ + 3 * DIFF_W
N_IN = O_DV_END + 2 * CONV_CH

PACK_W = 2 * CONV_CH + 3 * DIFF_W + 3 * FOX_W
PK_DIFF = 2 * CONV_CH
PK_FOX = PK_DIFF + 3 * DIFF_W
PROJ_W = PACK_W - 2 * CONV_CH
P_FOX = 3 * DIFF_W

LANES = 128
MXU_N = 256
ROW_TILE = 688
HALO = 16
PROJ_TN = 768
FFN_TN = 512
ATT_T = 256
FOX_G = 6
DIFF_G = 4
LOG2E = math.log2(math.e)
MASK_VALUE = -1e30
VMEM_LIMIT = 56 * 1024 * 1024

F32 = jnp.float32
BF16 = jnp.bfloat16


def _rms_rows(x, g, eps):
    return (x * lax.rsqrt(jnp.mean(x * x, axis=-1, keepdims=True) + eps)) * g


def _sigmoid(x):
    return 1.0 / (1.0 + jnp.exp(-x))


def _dot_nt(a, b):
    return lax.dot_general(a, b, (((1,), (1,)), ((), ())), preferred_element_type=F32)


PACK_ROWS = 256


def _pack_kernel(w_ref, wp_ref, wff_ref):
    dq = w_ref[0, :, O_DQ:O_DQ + DIFF_W] * (DIFF_SCALE * LOG2E)
    fq = w_ref[0, :, 0:FOX_W] * (FOX_SCALE * LOG2E)
    wp_ref[0, :, 0:PK_DIFF] = w_ref[0, :, O_DV_END:N_IN].astype(BF16)
    wp_ref[0, :, PK_DIFF:PK_DIFF + DIFF_W] = dq.astype(BF16)
    wp_ref[0, :, PK_DIFF + DIFF_W:PK_FOX] = w_ref[0, :, O_DQ + DIFF_W:O_DV_END].astype(BF16)
    wp_ref[0, :, PK_FOX:PK_FOX + FOX_W] = fq.astype(BF16)
    wp_ref[0, :, PK_FOX + FOX_W:PACK_W] = w_ref[0, :, FOX_W:O_FF].astype(BF16)
    ff = w_ref[0, :, O_FF:O_FF + LANES]
    lane = lax.broadcasted_iota(jnp.int32, ff.shape, 1)
    wff_ref[0] = jnp.where(lane < FOX_HEADS, ff, 0.0).astype(BF16)


def _pack_w_in(w_in):
    depth, d, n_in = w_in.shape
    assert n_in == N_IN
    return pl.pallas_call(
        _pack_kernel,
        grid=(depth, d // PACK_ROWS),
        in_specs=[pl.BlockSpec((1, PACK_ROWS, n_in), lambda l, r: (l, r, 0))],
        out_specs=[
            pl.BlockSpec((1, PACK_ROWS, PACK_W), lambda l, r: (l, r, 0)),
            pl.BlockSpec((1, PACK_ROWS, LANES), lambda l, r: (l, r, 0)),
        ],
        out_shape=[
            jax.ShapeDtypeStruct((depth, d, PACK_W), BF16),
            jax.ShapeDtypeStruct((depth, d, LANES), BF16),
        ],
        compiler_params=pltpu.CompilerParams(dimension_semantics=("arbitrary", "arbitrary")),
        name="pack_w_in",
    )(w_in)


SUBLANES = 8
CONV_PAD = 32
CONV_CHUNK = 16
CONV_TAP_GROUPS = -(-CONV_WIDTH // SUBLANES)
PROJ_STEPS = PACK_W // PROJ_TN


def _conv_chunk(u_ref, slot, r0, w_ref, b_ref, lng_ref, lnb_ref):
    win = CONV_CHUNK + SUBLANES
    cols_out = []
    for lc in range(CONV_CH // LANES):
        cols = slice(lc * LANES, (lc + 1) * LANES)
        xs = [u_ref[slot, pl.ds(r0 + CONV_PAD - SUBLANES * (a + 1), win), cols]
              for a in range(CONV_TAP_GROUPS)]
        acc = None
        for r in range(SUBLANES):
            part = None
            for a, x in enumerate(xs):
                s = SUBLANES * a + r
                if s >= CONV_WIDTH:
                    continue
                k = CONV_WIDTH - 1 - s
                term = x * w_ref[k:k + 1, cols]
                part = term if part is None else part + term
            if r:
                part = pltpu.roll(part, r, axis=0)
            part = part[SUBLANES:, :]
            acc = part if acc is None else acc + part
        cols_out.append(acc)
    u = jnp.concatenate(cols_out, axis=1) + b_ref[...]
    mu = jnp.mean(u, axis=-1, keepdims=True)
    uc = u - mu
    y = uc * lax.rsqrt(jnp.mean(uc * uc, axis=-1, keepdims=True) + EPS)
    y = y * lng_ref[...] + lnb_ref[...]
    return (y * _sigmoid(y)).astype(BF16)


def _inproj_kernel(h_ref, g_ref, w_ref, wff_ref, wdw_ref, bdw_ref, lng_ref, lnb_ref,
                   proj_ref, zf_ref, yc_ref, hn_ref, a_ref, u_ref, *, tiles_per_seq):
    i, j = pl.program_id(0), pl.program_id(1)
    slot = i % 2

    @pl.when(jnp.logical_and(i == 0, j == 0))
    def _():
        u_ref[...] = jnp.zeros_like(u_ref)

    @pl.when(j == 0)
    def _():
        hn = _rms_rows(h_ref[...], g_ref[...], EPS).astype(BF16)
        hn_ref[...] = hn
        zf_ref[...] = jnp.dot(hn, wff_ref[...], preferred_element_type=F32)

    chunks = -(-(ROW_TILE // CONV_CHUNK) // PROJ_STEPS)
    pieces = PROJ_TN // MXU_N
    done = 0
    for n in range(pieces):
        cols = slice(n * MXU_N, (n + 1) * MXU_N)
        part = jnp.dot(hn_ref[...], w_ref[:, cols], preferred_element_type=F32)
        proj_ref[:, cols] = part.astype(BF16)
        upto = chunks * (n + 1) // pieces
        for q in range(done, upto):
            c = jnp.minimum(j * chunks + q, ROW_TILE // CONV_CHUNK - 1)
            r0 = pl.multiple_of(c * CONV_CHUNK, CONV_CHUNK)
            yc_ref[pl.ds(r0, CONV_CHUNK), :] = _conv_chunk(
                u_ref, 1 - slot, r0, wdw_ref, bdw_ref, lng_ref, lnb_ref)
        done = upto

    @pl.when(j == 0)
    def _():
        a_ref[...] = proj_ref[...]

    @pl.when(j == 1)
    def _():
        keep = jnp.where(i % tiles_per_seq == 0, 0.0, 1.0)
        u_ref[slot, 0:CONV_PAD, :] = u_ref[1 - slot, ROW_TILE:ROW_TILE + CONV_PAD, :] * keep
        u_ref[slot, CONV_PAD:, :] = a_ref[...].astype(F32) * _sigmoid(proj_ref[...].astype(F32))


def _inproj(h, g, w_pack, w_ff, w_dw, b_dw, ln_g, ln_b, layer, seq):
    t = h.shape[0]
    n_tiles = t // ROW_TILE
    last = n_tiles - 1
    conv_steps = 2 * CONV_CH // PROJ_TN
    vec = pl.BlockSpec((1, CONV_CH), lambda i, j: (0, 0))
    rows = lambda i: jnp.minimum(i, last)
    return pl.pallas_call(
        functools.partial(_inproj_kernel, tiles_per_seq=seq // ROW_TILE),
        grid=(n_tiles + 1, PROJ_STEPS),
        in_specs=[
            pl.BlockSpec((ROW_TILE, D_MODEL), lambda i, j: (rows(i), 0)),
            pl.BlockSpec((1, D_MODEL), lambda i, j: (0, 0)),
            pl.BlockSpec((None, D_MODEL, PROJ_TN), lambda i, j: (layer, 0, j)),
            pl.BlockSpec((None, D_MODEL, LANES), lambda i, j: (layer, 0, 0)),
            pl.BlockSpec((None, CONV_WIDTH, CONV_CH), lambda i, j: (layer, 0, 0)),
            vec, vec, vec,
        ],
        out_specs=[
            pl.BlockSpec((ROW_TILE, PROJ_TN), lambda i, j: (i, jnp.maximum(j - conv_steps, 0))),
            pl.BlockSpec((ROW_TILE, LANES), lambda i, j: (i, 0)),
            pl.BlockSpec((ROW_TILE, CONV_CH), lambda i, j: (jnp.maximum(i - 1, 0), 0)),
        ],
        out_shape=[
            jax.ShapeDtypeStruct((t + ROW_TILE, PROJ_W), BF16),
            jax.ShapeDtypeStruct((t + ROW_TILE, LANES), F32),
            jax.ShapeDtypeStruct((t, CONV_CH), BF16),
        ],
        scratch_shapes=[
            pltpu.VMEM((ROW_TILE, D_MODEL), BF16),
            pltpu.VMEM((ROW_TILE, CONV_CH), BF16),
            pltpu.VMEM((2, CONV_PAD + ROW_TILE, CONV_CH), F32),
        ],
        compiler_params=pltpu.CompilerParams(
            dimension_semantics=("arbitrary", "arbitrary"), vmem_limit_bytes=VMEM_LIMIT),
        name="inproj",
    )(h, g, w_pack, w_ff, w_dw, b_dw, ln_g, ln_b)


def _row_cumsum(x):
    row = lax.broadcasted_iota(jnp.int32, x.shape, 0)
    d = 1
    while d < x.shape[0]:
        x = x + jnp.where(row >= d, pltpu.roll(x, d, axis=0), 0.0)
        d *= 2
    return x


def _forget_kernel(zf_ref, bf_ref, c_ref, *, seq):
    carry = jnp.zeros((1, LANES), F32)
    for r0 in range(0, seq, LANES):
        rows = min(LANES, seq - r0)
        z = zf_ref[r0:r0 + rows, :] + bf_ref[...]
        lf = jnp.minimum(z, 0.0) - jnp.log(1.0 + jnp.exp(-jnp.abs(z)))
        cs = _row_cumsum(lf) + carry
        c_ref[r0:r0 + rows, :] = cs
        carry = cs[rows - 1:rows, :]


def _forget_cumsum(zf, bf_pad, b, seq):
    blk = pl.BlockSpec((seq, LANES), lambda i: (i, 0))
    return pl.pallas_call(
        functools.partial(_forget_kernel, seq=seq),
        grid=(b,),
        in_specs=[blk, pl.BlockSpec((1, LANES), lambda i: (0, 0))],
        out_specs=blk,
        out_shape=jax.ShapeDtypeStruct((b * seq, LANES), F32),
        compiler_params=pltpu.CompilerParams(dimension_semantics=("arbitrary",)),
        name="forget_cumsum",
    )(zf, bf_pad)


def _softmax_step(pieces, m, l):
    ss = []
    for s, bias, mask in pieces:
        if bias is not None:
            s = s + bias
        if mask is not None:
            s = jnp.where(mask, s, MASK_VALUE)
        ss.append(s)
    m_new = m
    for s in ss:
        m_new = jnp.maximum(m_new, jnp.max(s, axis=0, keepdims=True))
    alpha = jnp.exp2(m - m_new)
    l = alpha * l
    ps = []
    for s in ss:
        p = jnp.exp2(s - m_new)
        l = l + jnp.sum(p, axis=0, keepdims=True)
        ps.append(p.astype(BF16))
    return m_new, l, alpha, ps


def _pv(vt, p):
    return jnp.dot(vt, p, preferred_element_type=F32)


def _causal_attention(n_chains, seq, tq, n_maps, q_of, k_of, vt_of, bias_of, emit, s_ref, acc_ref):
    first = seq % ATT_T
    assert 0 < first <= tq and (seq - first) % tq == 0 and ATT_T % tq == 0
    assert tq & (tq - 1) == 0
    n_qt = 1 + (seq - first) // tq
    lanes = tq * n_maps
    chains = range(n_chains)
    m0 = jnp.full((1, lanes), MASK_VALUE, F32)
    l0 = jnp.zeros((1, lanes), F32)

    def causal_mask(nk, q_off):
        key = lax.broadcasted_iota(jnp.int32, (nk, lanes), 0)
        qry = lax.broadcasted_iota(jnp.int32, (nk, lanes), 1) & (tq - 1)
        return key <= qry + q_off

    def scores(qs, k0, nk):
        return tuple(_dot_nt(k_of(g, k0, nk), qs[g]) for g in chains)

    qs = [q_of(g, 0) for g in chains]
    ss = scores(qs, 0, first)
    for g in chains:
        _, l, _, (p,) = _softmax_step([(ss[g], bias_of(g, 0, first), causal_mask(first, 0))],
                                      m0, l0)
        emit(g, 0, first, _pv(vt_of(g, 0, first), p) / l)

    def q_tile(i, _):
        done = (i - 1) * tq
        r0 = pl.multiple_of(first + done, HALO)
        n_before = done // ATT_T
        qs = [q_of(g, r0) for g in chains]
        s_first = scores(qs, 0, first)

        def put_scores(slot, k0):
            for g in chains:
                s_ref[slot, g] = _dot_nt(k_of(g, k0, ATT_T), qs[g])

        def step(slot, j, stats):
            put_scores(1 - slot, pl.multiple_of(first + (j + 1) * ATT_T, HALO))
            out = []
            for g in chains:
                m, l = stats[g]
                m, l, alpha, (p,) = _softmax_step(
                    [(s_ref[slot, g], bias_of(g, j + 1, ATT_T), None)], m, l)
                acc_ref[g] = alpha * acc_ref[g] + _pv(vt_of(g, j + 1, ATT_T), p)
                out.append((m, l))
            return tuple(out)

        def pair(p, stats):
            return step(1, 2 * p + 1, step(0, 2 * p, stats))

        put_scores(0, first)
        for g in chains:
            acc_ref[g] = jnp.zeros((HEAD_DIM, lanes), F32)
        stats = lax.fori_loop(0, n_before // 2, pair, tuple((m0, l0) for _ in chains))
        stats = lax.cond(n_before % 2 == 1, lambda st: step(0, n_before - 1, st), lambda st: st,
                         stats)
        diag = n_before + 1
        mask = causal_mask(ATT_T, done % ATT_T)
        for g in chains:
            m, l = stats[g]
            m, l, alpha, (p_d, p_f) = _softmax_step(
                [(s_ref[n_before % 2, g], bias_of(g, diag, ATT_T), mask),
                 (s_first[g], bias_of(g, 0, first), None)], m, l)
            acc = (alpha * acc_ref[g] + _pv(vt_of(g, diag, ATT_T), p_d)
                   + _pv(vt_of(g, 0, first), p_f))
            emit(g, r0, tq, acc / l)
        return 0

    lax.fori_loop(1, n_qt, q_tile, 0)


def _head_cols(g):
    return slice(g * HEAD_DIM, (g + 1) * HEAD_DIM)


def _key_tiles(seq):
    first = seq % ATT_T
    return [(0, first)] + [(r0, ATT_T) for r0 in range(first, seq, ATT_T)]


def _fill_vt(v_ref, vt_ref, heads, seq):
    for g in range(heads):
        for kt, (r0, nk) in enumerate(_key_tiles(seq)):
            v = v_ref[r0:r0 + nk, _head_cols(g)].astype(F32)
            if nk < LANES:
                v = jnp.concatenate([v, jnp.zeros((LANES - nk, HEAD_DIM), F32)], axis=0)
            vt_ref[g, kt, :, 0:v.shape[0]] = v.T.astype(BF16)


def _vt_of(vt_ref):
    def vt_of(g, kt, nk):
        return vt_ref[g, kt] if nk == ATT_T else vt_ref[g, kt, :, 0:nk]
    return vt_of


def _fox_kernel(q_ref, k_ref, v_ref, c_ref, o_ref, vt_ref, bias_ref, s_ref, acc_ref,
                *, seq, heads):
    head0 = pl.program_id(1) * heads
    _fill_vt(v_ref, vt_ref, heads, seq)
    for kt, (r0, nk) in enumerate(_key_tiles(seq)):
        c = c_ref[r0:r0 + nk, :]
        lane = lax.broadcasted_iota(jnp.int32, c.shape, 1)
        for g in range(heads):
            col = jnp.sum(jnp.where(lane == head0 + g, c, 0.0), axis=1, keepdims=True)
            bias_ref[g, kt, 0:nk, :] = jnp.broadcast_to(col * (-LOG2E), (nk, LANES))

    def q_of(g, r0):
        return q_ref[pl.ds(r0, ATT_T), _head_cols(g)]

    def k_of(g, k0, nk):
        return k_ref[pl.ds(k0, nk), _head_cols(g)]

    def bias_of(g, kt, nk):
        b = bias_ref[g, kt] if nk == ATT_T else bias_ref[g, kt, 0:nk, :]
        return jnp.concatenate([b] * (ATT_T // LANES), axis=1)

    def emit(g, r0, n_rows, o):
        o_ref[pl.ds(r0, n_rows), _head_cols(g)] = o.T[0:n_rows].astype(BF16)

    _causal_attention(heads, seq, ATT_T, 1, q_of, k_of, _vt_of(vt_ref), bias_of, emit,
                      s_ref, acc_ref)


def _fox_attention(proj, c, b, seq):
    n_kt = len(_key_tiles(seq))
    width = FOX_G * HEAD_DIM
    blk = (seq, width)
    base = P_FOX // width
    per = FOX_W // width
    return pl.pallas_call(
        functools.partial(_fox_kernel, seq=seq, heads=FOX_G),
        grid=(b, per),
        in_specs=[
            pl.BlockSpec(blk, lambda i, h: (i, base + h)),
            pl.BlockSpec(blk, lambda i, h: (i, base + per + h)),
            pl.BlockSpec(blk, lambda i, h: (i, base + 2 * per + h)),
            pl.BlockSpec((seq, LANES), lambda i, h: (i, 0)),
        ],
        out_specs=pl.BlockSpec(blk, lambda i, h: (i, h)),
        out_shape=jax.ShapeDtypeStruct((b * seq, FOX_W), BF16),
        scratch_shapes=[
            pltpu.VMEM((FOX_G, n_kt, HEAD_DIM, ATT_T), BF16),
            pltpu.VMEM((FOX_G, n_kt, ATT_T, LANES), F32),
            pltpu.VMEM((2, FOX_G, ATT_T, ATT_T), F32),
            pltpu.VMEM((FOX_G, HEAD_DIM, ATT_T), F32),
        ],
        compiler_params=pltpu.CompilerParams(
            dimension_semantics=("arbitrary", "arbitrary"), vmem_limit_bytes=VMEM_LIMIT),
        name="fox_attention",
    )(proj, proj, proj, c)


DIFF_TQ = ATT_T // 2


def _diff_kernel(q_ref, k_ref, v_ref, lq1_ref, lk1_ref, lq2_ref, lk2_ref, gsub_ref, o_ref, vt_ref,
                 s_ref, acc_ref, *, seq, heads, lam_init):
    lam = (jnp.exp(jnp.sum(lq1_ref[...] * lk1_ref[...], axis=-1, keepdims=True))
           - jnp.exp(jnp.sum(lq2_ref[...] * lk2_ref[...], axis=-1, keepdims=True))
           + lam_init)
    _fill_vt(v_ref, vt_ref, heads, seq)

    def q_of(g, r0):
        q = q_ref[pl.ds(r0, DIFF_TQ), _head_cols(g)]
        lane = lax.broadcasted_iota(jnp.int32, q.shape, 1)
        zero = jnp.zeros_like(q)
        return jnp.concatenate([jnp.where(lane < DIFF_QK, q, zero),
                                jnp.where(lane >= DIFF_QK, q, zero)], axis=0)

    def k_of(g, k0, nk):
        return k_ref[pl.ds(k0, nk), _head_cols(g)]

    def emit(g, r0, n_rows, o):
        y = (o[:, 0:DIFF_TQ] - lam * o[:, DIFF_TQ:2 * DIFF_TQ]).T[0:n_rows]
        y = _rms_rows(y, gsub_ref[...], SUBLN_EPS) * (1.0 - lam_init)
        o_ref[pl.ds(r0, n_rows), _head_cols(g)] = y.astype(BF16)

    _causal_attention(heads, seq, DIFF_TQ, 2, q_of, k_of, _vt_of(vt_ref),
                      lambda g, kt, nk: None, emit, s_ref, acc_ref)


def _diff_attention(proj, lq1, lk1, lq2, lk2, g_sub, lam_init, b, seq):
    n_kt = len(_key_tiles(seq))
    width = DIFF_G * HEAD_DIM
    blk = (seq, width)
    per = DIFF_W // width
    vec = pl.BlockSpec((1, DIFF_QK), lambda i, h: (0, 0))
    return pl.pallas_call(
        functools.partial(_diff_kernel, seq=seq, heads=DIFF_G, lam_init=lam_init),
        grid=(b, per),
        in_specs=[
            pl.BlockSpec(blk, lambda i, h: (i, h)),
            pl.BlockSpec(blk, lambda i, h: (i, per + h)),
            pl.BlockSpec(blk, lambda i, h: (i, 2 * per + h)),
            vec, vec, vec, vec,
            pl.BlockSpec((1, DIFF_V), lambda i, h: (0, 0)),
        ],
        out_specs=pl.BlockSpec(blk, lambda i, h: (i, h)),
        out_shape=jax.ShapeDtypeStruct((b * seq, DIFF_W), BF16),
        scratch_shapes=[
            pltpu.VMEM((DIFF_G, n_kt, HEAD_DIM, ATT_T), BF16),
            pltpu.VMEM((2, DIFF_G, ATT_T, 2 * DIFF_TQ), F32),
            pltpu.VMEM((DIFF_G, HEAD_DIM, 2 * DIFF_TQ), F32),
        ],
        compiler_params=pltpu.CompilerParams(
            dimension_semantics=("arbitrary", "arbitrary"), vmem_limit_bytes=VMEM_LIMIT),
        name="diff_attention",
    )(proj, proj, proj, lq1, lk1, lq2, lk2, g_sub)


def _outproj_kernel(yf_ref, yd_ref, yc_ref, w_ref, h_ref, g_ref, o_ref):
    m = jnp.dot(yf_ref[...], w_ref[0:FOX_W, :], preferred_element_type=F32)
    m = m + jnp.dot(yd_ref[...], w_ref[FOX_W:FOX_W + DIFF_W, :], preferred_element_type=F32)
    m = m + jnp.dot(yc_ref[...], w_ref[FOX_W + DIFF_W:, :], preferred_element_type=F32)
    o_ref[...] = h_ref[...] + _rms_rows(m, g_ref[...], EPS)


def _outproj(y_fox, y_diff, y_conv, w_out, h, g):
    t = h.shape[0]
    rows = lambda w: pl.BlockSpec((ROW_TILE, w), lambda i: (i, 0))
    return pl.pallas_call(
        _outproj_kernel,
        grid=(t // ROW_TILE,),
        in_specs=[
            rows(FOX_W), rows(DIFF_W), rows(CONV_CH),
            pl.BlockSpec((D_MODEL, D_MODEL), lambda i: (0, 0)),
            rows(D_MODEL),
            pl.BlockSpec((1, D_MODEL), lambda i: (0, 0)),
        ],
        out_specs=rows(D_MODEL),
        out_shape=jax.ShapeDtypeStruct((t, D_MODEL), F32),
        compiler_params=pltpu.CompilerParams(
            dimension_semantics=("arbitrary",), vmem_limit_bytes=VMEM_LIMIT),
        name="outproj",
    )(y_fox, y_diff, y_conv, w_out, h, g)


def _ffn_kernel(h_ref, hp_ref, gpre_ref, wg_ref, wu_ref, wc_ref, wd_ref, gpost_ref, o_ref,
                hn_ref, gate_ref, acc_ref, *, tiles_per_seq):
    i, j = pl.program_id(0), pl.program_id(1)

    @pl.when(j == 0)
    def _():
        keep = jnp.where(i % tiles_per_seq == 0, 0.0, 1.0)
        hn_ref[0:HALO, :] = (_rms_rows(hp_ref[...], gpre_ref[...], EPS) * keep).astype(BF16)
        hn_ref[HALO:, :] = _rms_rows(h_ref[...], gpre_ref[...], EPS).astype(BF16)
        acc_ref[...] = jnp.zeros_like(acc_ref)

    gate_ref[...] = jnp.dot(hn_ref[...], wg_ref[...], preferred_element_type=F32)
    up = jnp.dot(hn_ref[HALO:, :], wu_ref[...], preferred_element_type=F32)
    conv = gate_ref[HALO:, :] * wc_ref[FFN_CONV - 1:FFN_CONV, :]
    for k in range(FFN_CONV - 1):
        back = FFN_CONV - 1 - k
        conv = conv + gate_ref[HALO - back:HALO - back + ROW_TILE, :] * wc_ref[k:k + 1, :]
    act = (conv * _sigmoid(conv) * up).astype(BF16)
    acc_ref[...] += jnp.dot(act, wd_ref[...], preferred_element_type=F32)

    @pl.when(j == pl.num_programs(1) - 1)
    def _():
        o_ref[...] = h_ref[...] + _rms_rows(acc_ref[...], gpost_ref[...], EPS)


def _ffn(h, g_pre, w_gate, w_up, w_conv, w_down, g_post, seq):
    t = h.shape[0]
    per = ROW_TILE // HALO
    return pl.pallas_call(
        functools.partial(_ffn_kernel, tiles_per_seq=seq // ROW_TILE),
        grid=(t // ROW_TILE, D_FF // FFN_TN),
        in_specs=[
            pl.BlockSpec((ROW_TILE, D_MODEL), lambda i, j: (i, 0)),
            pl.BlockSpec((HALO, D_MODEL), lambda i, j: (jnp.maximum(i * per - 1, 0), 0)),
            pl.BlockSpec((1, D_MODEL), lambda i, j: (0, 0)),
            pl.BlockSpec((D_MODEL, FFN_TN), lambda i, j: (0, j)),
            pl.BlockSpec((D_MODEL, FFN_TN), lambda i, j: (0, j)),
            pl.BlockSpec((FFN_CONV, FFN_TN), lambda i, j: (0, j)),
            pl.BlockSpec((FFN_TN, D_MODEL), lambda i, j: (j, 0)),
            pl.BlockSpec((1, D_MODEL), lambda i, j: (0, 0)),
        ],
        out_specs=pl.BlockSpec((ROW_TILE, D_MODEL), lambda i, j: (i, 0)),
        out_shape=jax.ShapeDtypeStruct((t, D_MODEL), F32),
        scratch_shapes=[
            pltpu.VMEM((HALO + ROW_TILE, D_MODEL), BF16),
            pltpu.VMEM((HALO + ROW_TILE, FFN_TN), F32),
            pltpu.VMEM((ROW_TILE, D_MODEL), F32),
        ],
        compiler_params=pltpu.CompilerParams(
            dimension_semantics=("arbitrary", "arbitrary"), vmem_limit_bytes=VMEM_LIMIT),
        name="ffn",
    )(h, h, g_pre, w_gate, w_up, w_conv, w_down, g_post)


def kernel(x, meta_tokens, w_in, b_f, lam_q1, lam_k1, lam_q2, lam_k2, g_sub, w_dw, b_dw, ln_g,
           ln_b, w_out, w_gate, w_up, w_ffn_conv, w_down, g_pre_mix, g_post_mix, g_pre_ffn,
           g_post_ffn):
    bsz, _, d = x.shape
    depth = w_in.shape[0]
    meta = jnp.broadcast_to(meta_tokens.astype(x.dtype)[None], (bsz, N_META, d))
    h = jnp.concatenate([meta, x], axis=1)
    seq = h.shape[1]
    assert seq % ROW_TILE == 0 and ROW_TILE % HALO == 0
    h = h.reshape(bsz * seq, d)

    row = lambda v: v.reshape(1, -1)
    w_pack, w_ff = _pack_w_in(w_in)
    for layer in range(depth):
        bf_pad = jnp.pad(b_f[layer], (0, LANES - FOX_HEADS)).reshape(1, LANES)
        lam_init = 0.8 - 0.6 * math.exp(-0.3 * layer)

        proj, zf, y_conv = _inproj(h, row(g_pre_mix[layer]), w_pack, w_ff, w_dw,
                                   row(b_dw[layer]), row(ln_g[layer]), row(ln_b[layer]), layer, seq)
        c = _forget_cumsum(zf, bf_pad, bsz, seq)
        y_fox = _fox_attention(proj, c, bsz, seq)
        y_diff = _diff_attention(proj, row(lam_q1[layer]), row(lam_k1[layer]), row(lam_q2[layer]),
                                 row(lam_k2[layer]), row(g_sub[layer]), lam_init, bsz, seq)
        h = _outproj(y_fox, y_diff, y_conv, w_out[layer].astype(BF16), h, row(g_post_mix[layer]))
        h = _ffn(h, row(g_pre_ffn[layer]), w_gate[layer].astype(BF16), w_up[layer].astype(BF16),
                 w_ffn_conv[layer], w_down[layer].astype(BF16), row(g_post_ffn[layer]), seq)

    return h.reshape(bsz, seq, d)[:, N_META:]
```
